```python
import math
import jax
import jax.numpy as jnp
from jax import lax
import numpy as np

D_MODEL = 1024
BATCH = 4
SEQ = 4096
DEPTH = 1

EPS = 1e-5
SSD_EXPAND = 2
D_SSD = SSD_EXPAND * D_MODEL
SSD_HEAD_DIM = 64
SSD_HEADS = D_SSD // SSD_HEAD_DIM
SSD_GROUPS = 4
SSD_STATE = 128
SSD_CONV = 4
SSD_CHUNK = 128
D_CONV_IN = D_SSD + 2 * SSD_GROUPS * SSD_STATE
POOL_WINDOWS = (2, 4, 8, 16)
N_POOL_GROUPS = len(POOL_WINDOWS)
D_POOL = D_MODEL
POOL_GROUP_DIM = D_POOL // N_POOL_GROUPS
N_BRANCHES = 2
IN_SPLITS = (D_SSD, D_CONV_IN, SSD_HEADS, D_POOL, N_BRANCHES * D_MODEL)
D_IN_PROJ = sum(IN_SPLITS)
N_EXPERTS = 32
TOP_K = 4
D_EXPERT = D_MODEL
SWIGLU_LIMIT = 7.0
SWIGLU_ALPHA = 1.702
MOE_BLOCK = 128

kernel_name = 'hybrid_ssd_pool_moe_adaln_block'


def rms_norm(x, w):
    xf = x.astype(jnp.float32)
    y = xf * lax.rsqrt(jnp.mean(xf * xf, axis=-1, keepdims=True) + EPS)
    return (y * w.astype(jnp.float32)).astype(x.dtype)


def causal_depthwise_conv(u, w, b):
    k, ch = w.shape
    out = lax.conv_general_dilated(u, w[:, None, :], window_strides=(1,), padding=[(k - 1, 0)],
                                   dimension_numbers=('NWC', 'WIO', 'NWC'), feature_group_count=ch)
    return out + b


def ssd_chunked(xh, dt, A, Bm, Cm):
    f32 = jnp.float32
    b, s = xh.shape[0], xh.shape[1]
    nc, L = s // SSD_CHUNK, SSD_CHUNK
    G, E, P, N = SSD_GROUPS, SSD_HEADS // SSD_GROUPS, SSD_HEAD_DIM, SSD_STATE
    X = (xh.astype(f32) * dt[..., None]).reshape(b, nc, L, G, E, P)
    a = jnp.transpose((dt * A).reshape(b, nc, L, G, E), (0, 1, 3, 4, 2))
    Bc = Bm.astype(f32).reshape(b, nc, L, G, N)
    Cc = Cm.astype(f32).reshape(b, nc, L, G, N)
    a_cum = jnp.cumsum(a, axis=-1)
    causal = jnp.tril(jnp.ones((L, L), dtype=bool))
    seg = a_cum[..., :, None] - a_cum[..., None, :]
    decay = jnp.exp(jnp.where(causal, seg, -jnp.inf))
    CB = jnp.einsum('bclgn,bcsgn->bcgls', Cc, Bc)
    y_diag = jnp.einsum('bcgls,bcgels,bcsgep->bclgep', CB, decay, X)
    decay_to_end = jnp.exp(a_cum[..., -1:] - a_cum)
    states = jnp.einsum('bclgn,bcgel,bclgep->bcgepn', Bc, decay_to_end, X)
    chunk_decay = jnp.exp(a_cum[..., -1])

    def step(h, inp):
        st, dec = inp
        return h * dec[..., None, None] + st, h

    h0 = jnp.zeros((b, G, E, P, N), f32)
    _, prev = lax.scan(step, h0, (jnp.moveaxis(states, 1, 0), jnp.moveaxis(chunk_decay, 1, 0)))
    prev = jnp.moveaxis(prev, 0, 1)
    y_off = jnp.einsum('bclgn,bcgepn,bcgel->bclgep', Cc, prev, jnp.exp(a_cum))
    return (y_diag + y_off).reshape(b, s, SSD_HEADS, P)


def multiscale_causal_pool(u):
    f32 = jnp.float32
    b, s, ch = u.shape
    uf = u.astype(f32)
    cs = jnp.concatenate([jnp.zeros((b, 1, ch), f32), jnp.cumsum(uf, axis=1)], axis=1)
    t = jnp.arange(s)
    outs = []
    for gi, w in enumerate(POOL_WINDOWS):
        lo, hi = gi * POOL_GROUP_DIM, (gi + 1) * POOL_GROUP_DIM
        csg = cs[:, :, lo:hi]
        upper = csg[:, 1:]
        lower = jnp.pad(csg[:, :s + 1 - w], ((0, 0), (w - 1, 0), (0, 0)))
        cnt = jnp.minimum(t + 1, w).astype(f32)[None, :, None]
        outs.append((upper - lower) / cnt - uf[:, :, lo:hi])
    return jnp.concatenate(outs, axis=-1).astype(u.dtype)


def hybrid_mixer(h, w_in, conv_w, conv_b, dt_bias, a_log, d_skip, ssd_norm_w, w_ssd_out,
                 w_pool, pool_scale, w_pool_out, w_out):
    f32 = jnp.float32
    b, s, _ = h.shape
    proj = h @ w_in
    z, xbc, dt_raw, u_pool, gates = jnp.split(proj, list(np.cumsum(IN_SPLITS)[:-1]), axis=-1)
    xbc = jax.nn.silu(causal_depthwise_conv(xbc, conv_w, conv_b))
    xs, Bm, Cm = jnp.split(xbc, [D_SSD, D_SSD + SSD_GROUPS * SSD_STATE], axis=-1)
    dt = jax.nn.softplus(dt_raw.astype(f32) + dt_bias.astype(f32))
    A = -jnp.exp(a_log.astype(f32))
    xh = xs.reshape(b, s, SSD_HEADS, SSD_HEAD_DIM)
    y = ssd_chunked(xh, dt, A, Bm.reshape(b, s, SSD_GROUPS, SSD_STATE),
                    Cm.reshape(b, s, SSD_GROUPS, SSD_STATE))
    y = y + d_skip.astype(f32)[:, None] * xh.astype(f32)
    y = y.reshape(b, s, D_SSD) * jax.nn.silu(z.astype(f32))
    yg = y.reshape(b, s, SSD_GROUPS, D_SSD // SSD_GROUPS)
    yg = yg * lax.rsqrt(jnp.mean(yg * yg, axis=-1, keepdims=True) + EPS)
    y = (yg.reshape(b, s, D_SSD) * ssd_norm_w.astype(f32)).astype(h.dtype)
    y_ssd = y @ w_ssd_out
    p = multiscale_causal_pool(u_pool).reshape(b, s, N_POOL_GROUPS, POOL_GROUP_DIM)
    p = jnp.einsum('bsgc,gcd->bsgd', p, w_pool).reshape(b, s, D_POOL) * pool_scale
    y_pool = p @ w_pool_out
    g = jax.nn.sigmoid(gates.astype(f32)).astype(h.dtype)
    g_ssd, g_pool = jnp.split(g, N_BRANCHES, axis=-1)
    return (g_ssd * y_ssd + g_pool * y_pool) @ w_out


def moe_ffn(h, w_router, b_router, w_gu, b_gu, w_down, b_down):
    f32 = jnp.float32
    b, s, d = h.shape
    T = b * s
    xt = h.reshape(T, d)
    logits = xt.astype(f32) @ w_router.astype(f32) + b_router.astype(f32)
    top_val, top_idx = lax.top_k(logits, TOP_K)
    gate = jax.nn.softmax(top_val, axis=-1)
    e_flat = top_idx.reshape(-1).astype(jnp.int32)
    tok_flat = jnp.arange(T * TOP_K, dtype=jnp.int32) // TOP_K
    g_flat = gate.reshape(-1)
    order = jnp.argsort(e_flat)
    e_sorted = e_flat[order]
    counts = jnp.bincount(e_flat, length=N_EXPERTS).astype(jnp.int32)
    starts = jnp.cumsum(counts) - counts
    pad_counts = (counts + MOE_BLOCK - 1) // MOE_BLOCK * MOE_BLOCK
    pad_ends = jnp.cumsum(pad_counts)
    pad_starts = pad_ends - pad_counts
    dest = pad_starts[e_sorted] + (jnp.arange(T * TOP_K, dtype=jnp.int32) - starts[e_sorted])
    n_blocks = (T * TOP_K + MOE_BLOCK - 1) // MOE_BLOCK + N_EXPERTS
    n_rows = n_blocks * MOE_BLOCK
    row_tok = jnp.full((n_rows,), T, jnp.int32).at[dest].set(tok_flat[order])
    row_gate = jnp.zeros((n_rows,), f32).at[dest].set(g_flat[order])
    block_exp = jnp.minimum(jnp.searchsorted(pad_ends, jnp.arange(n_blocks, dtype=jnp.int32) * MOE_BLOCK,
                                             side='right'), N_EXPERTS - 1)
    x_pad = jnp.concatenate([xt, jnp.zeros((1, d), xt.dtype)], axis=0)
    xb = x_pad[row_tok].reshape(n_blocks, MOE_BLOCK, d)

    def expert_block(args):
        xblk, e = args
        gu = xblk @ w_gu[e] + b_gu[e]
        glu, lin = jnp.split(gu, 2, axis=-1)
        glu = jnp.minimum(glu, SWIGLU_LIMIT)
        lin = jnp.clip(lin, -SWIGLU_LIMIT, SWIGLU_LIMIT)
        act = glu * jax.nn.sigmoid(SWIGLU_ALPHA * glu) * (lin + 1.0)
        return act @ w_down[e] + b_down[e]

    yb = lax.map(expert_block, (xb, block_exp)).reshape(n_rows, d)
    y = jax.ops.segment_sum(yb.astype(f32) * row_gate[:, None], row_tok, num_segments=T + 1)[:T]
    return y.reshape(b, s, d).astype(h.dtype)


def setup_inputs(seed: int = 0) -> dict:
    key = jax.random.key(seed)
    ks = jax.random.split(key, 28)
    f32 = jnp.float32
    L = DEPTH

    def nrm(k, shape, scale):
        return jax.random.normal(k, shape, f32) * scale

    dt0 = jnp.exp(jax.random.uniform(ks[8], (L, SSD_HEADS), f32, minval=math.log(1e-3), maxval=math.log(1e-1)))
    return {
        'x': nrm(ks[0], (BATCH, SEQ, D_MODEL), 1.0),
        'c': nrm(ks[1], (BATCH, D_MODEL), 1.0),
        'w_ada': nrm(ks[2], (L, D_MODEL, 6 * D_MODEL), 0.5 * D_MODEL ** -0.5),
        'b_ada': nrm(ks[3], (L, 6 * D_MODEL), 0.02),
        'norm_mix_w': 1.0 + nrm(ks[4], (L, D_MODEL), 0.02),
        'w_in': nrm(ks[5], (L, D_MODEL, D_IN_PROJ), D_MODEL ** -0.5),
        'conv_w': nrm(ks[6], (L, SSD_CONV, D_CONV_IN), SSD_CONV ** -0.5),
        'conv_b': nrm(ks[7], (L, D_CONV_IN), 0.02),
        'dt_bias': dt0 + jnp.log(-jnp.expm1(-dt0)),
        'a_log': jnp.log(jax.random.uniform(ks[9], (L, SSD_HEADS), f32, minval=1.0, maxval=16.0)),
        'd_skip': 1.0 + nrm(ks[10], (L, SSD_HEADS), 0.1),
        'ssd_norm_w': 1.0 + nrm(ks[11], (L, D_SSD), 0.02),
        'w_ssd_out': nrm(ks[12], (L, D_SSD, D_MODEL), D_SSD ** -0.5),
        'w_pool': nrm(ks[13], (L, N_POOL_GROUPS, POOL_GROUP_DIM, POOL_GROUP_DIM), POOL_GROUP_DIM ** -0.5),
        'pool_scale': 1.0 + nrm(ks[14], (L, D_POOL), 0.1),
        'w_pool_out': nrm(ks[15], (L, D_POOL, D_MODEL), D_POOL ** -0.5),
        'w_out': nrm(ks[16], (L, D_MODEL, D_MODEL), D_MODEL ** -0.5),
        'norm_ffn_w': 1.0 + nrm(ks[17], (L, D_MODEL), 0.02),
        'w_router': nrm(ks[18], (L, D_MODEL, N_EXPERTS), D_MODEL ** -0.5),
        'b_router': nrm(ks[19], (L, N_EXPERTS), 0.01),
        'w_gu': nrm(ks[20], (L, N_EXPERTS, D_MODEL, 2 * D_EXPERT), D_MODEL ** -0.5),
        'b_gu': nrm(ks[21], (L, N_EXPERTS, 2 * D_EXPERT), 0.02),
        'w_down': nrm(ks[22], (L, N_EXPERTS, D_EXPERT, D_MODEL), D_EXPERT ** -0.5),
        'b_down': nrm(ks[23], (L, N_EXPERTS, D_MODEL), 0.02),
        'norm_final_w': 1.0 + nrm(ks[24], (D_MODEL,), 0.02),
    }


def reference(x, c, w_ada, b_ada, norm_mix_w, w_in, conv_w, conv_b, dt_bias, a_log, d_skip,
              ssd_norm_w, w_ssd_out, w_pool, pool_scale, w_pool_out, w_out, norm_ffn_w,
              w_router, b_router, w_gu, b_gu, w_down, b_down, norm_final_w):
    c_act = jax.nn.silu(c)
    for l in range(DEPTH):
        mod = (c_act @ w_ada[l] + b_ada[l])[:, None, :]
        sh_m, sc_m, ga_m, sh_f, sc_f, ga_f = jnp.split(mod, 6, axis=-1)
        h = rms_norm(x, norm_mix_w[l]) * (1.0 + sc_m) + sh_m
        x = x + ga_m * hybrid_mixer(h, w_in[l], conv_w[l], conv_b[l], dt_bias[l], a_log[l], d_skip[l],
                                    ssd_norm_w[l], w_ssd_out[l], w_pool[l], pool_scale[l],
                                    w_pool_out[l], w_out[l])
        h = rms_norm(x, norm_ffn_w[l]) * (1.0 + sc_f) + sh_f
        x = x + ga_f * moe_ffn(h, w_router[l], b_router[l], w_gu[l], b_gu[l], w_down[l], b_down[l])
    return rms_norm(x, norm_final_w)
```

```python
import functools

import jax
import jax.numpy as jnp
from jax import lax
from jax.experimental import pallas as pl
from jax.experimental.pallas import tpu as pltpu

F32 = jnp.float32
BF16 = jnp.bfloat16
I32 = jnp.int32
HIGHEST = lax.Precision.HIGHEST

D = 1024
NB = 4
S = 4096
T = NB * S
EPS = 1e-5
D_SSD = 2048
N_HEADS = 32
HEAD_DIM = 64
N_GROUPS = 4
D_STATE = 128
GROUP_W = D_SSD // N_GROUPS
D_XBC = D_SSD + 2 * N_GROUPS * D_STATE
CONV_K = 4
POOL_WINDOWS = (2, 4, 8, 16)
POOL_GW = 256
N_EXP = 32
TOP_K = 4
SWIGLU_LIMIT = 7.0
SWIGLU_ALPHA = 1.702

LANES = 128
SUBLANES = 8

CHUNK = 128
TM_PROJ = 512
TM_MERGE = 512
TM_FINAL = 512
BM = 128
N_SLOTS = T * TOP_K
N_BLOCKS = N_SLOTS // BM + N_EXP
HALO = 16
SORT_ROWS = N_SLOTS // LANES

VMEM_LIMIT = 56 * 1024 * 1024


def _sigmoid(v):
    return 1.0 / (1.0 + jnp.exp(-v))


def _softplus(v):
    return jnp.maximum(v, 0.0) + jnp.log(1.0 + jnp.exp(-jnp.abs(v)))


def _resident(shape):
    nd = len(shape)
    return pl.BlockSpec(shape, lambda *_: (0,) * nd, pipeline_mode=pl.Buffered(1))


def _adaln_kernel(c_ref, w_ref, b_ref, o_ref):
    c = c_ref[...]
    act = c * _sigmoid(c)
    o_ref[...] = jnp.dot(act, w_ref[...], precision=HIGHEST,
                         preferred_element_type=F32) + b_ref[...]


def _adaln(c, w_ada, b_ada):
    tn = 1536
    return pl.pallas_call(
        _adaln_kernel,
        grid=(6 * D // tn,),
        in_specs=[pl.BlockSpec((NB, D), lambda j: (0, 0)),
                  pl.BlockSpec((D, tn), lambda j: (0, j)),
                  pl.BlockSpec((1, tn), lambda j: (0, j))],
        out_specs=pl.BlockSpec((NB, tn), lambda j: (0, j)),
        out_shape=jax.ShapeDtypeStruct((NB, 6 * D), F32),
        compiler_params=pltpu.CompilerParams(vmem_limit_bytes=VMEM_LIMIT),
        name="adaln",
    )(c, w_ada, b_ada)


def _inproj_kernel(x_ref, mod_ref, nw_ref, wz_ref, wx_ref, wdt_ref, wp_ref, wg_ref,
                   z_ref, xbc_ref, dt_ref, up_ref, g_ref):
    x = x_ref[0]
    ms = jnp.mean(x * x, axis=-1, keepdims=True)
    h = x * lax.rsqrt(ms + EPS) * nw_ref[...]
    h = h * (1.0 + mod_ref[0, 1:2, :]) + mod_ref[0, 0:1, :]
    hb = h.astype(BF16)
    z_ref[0] = jnp.dot(hb, wz_ref[...], preferred_element_type=F32).astype(BF16)
    xbc_ref[0] = jnp.dot(hb, wx_ref[...], preferred_element_type=F32).astype(BF16)
    dt_ref[0] = jnp.dot(hb, wdt_ref[...], preferred_element_type=F32)
    up_ref[0] = jnp.dot(hb, wp_ref[...], preferred_element_type=F32).astype(BF16)
    g_ref[0] = jnp.dot(hb, wg_ref[...], preferred_element_type=F32).astype(BF16)


def _inproj(x, mod, norm_w, wz, wx, wdt, wp, wg):
    tm = TM_PROJ
    tok = lambda n: pl.BlockSpec((1, tm, n), lambda b, i: (b, i, 0))
    return pl.pallas_call(
        _inproj_kernel,
        grid=(NB, S // tm),
        in_specs=[tok(D),
                  pl.BlockSpec((1, 6, D), lambda b, i: (b, 0, 0)),
                  _resident((1, D)),
                  _resident((D, D_SSD)), _resident((D, D_XBC)), _resident((D, LANES)),
                  _resident((D, D)), _resident((D, 2 * D))],
        out_specs=[tok(D_SSD), tok(D_XBC), tok(LANES), tok(D), tok(2 * D)],
        out_shape=[jax.ShapeDtypeStruct((NB, S, D_SSD), BF16),
                   jax.ShapeDtypeStruct((NB, S, D_XBC), BF16),
                   jax.ShapeDtypeStruct((NB, S, LANES), F32),
                   jax.ShapeDtypeStruct((NB, S, D), BF16),
                   jax.ShapeDtypeStruct((NB, S, 2 * D), BF16)],
        compiler_params=pltpu.CompilerParams(
            dimension_semantics=("arbitrary", "arbitrary"), vmem_limit_bytes=VMEM_LIMIT),
        name="inproj",
    )(x, mod, norm_w, wz, wx, wdt, wp, wg)


def _ssd_kernel(xbc_ref, z_ref, dtr_ref, cw_ref, cb_ref, dtb_row_ref, alog_row_ref,
                dtb_col_ref, alog_col_ref, dskip_ref, nw_ref, expand_ref,
                y_ref, ubuf, state, ybuf):
    c = pl.program_id(1)

    @pl.when(c == 0)
    def _():
        ubuf[0:SUBLANES, :] = jnp.zeros((SUBLANES, D_XBC), F32)
        state[...] = jnp.zeros_like(state)

    ubuf[SUBLANES:SUBLANES + CHUNK, :] = xbc_ref[0].astype(F32)
    conv = cb_ref[...] + cw_ref[0:1, :] * ubuf[pl.ds(SUBLANES - 3, CHUNK), :]
    for k in range(1, CONV_K):
        conv = conv + cw_ref[k:k + 1, :] * ubuf[pl.ds(SUBLANES - 3 + k, CHUNK), :]
    ubuf[0:SUBLANES, :] = ubuf[CHUNK:CHUNK + SUBLANES, :]
    xc = conv * _sigmoid(conv)

    row = lax.broadcasted_iota(I32, (CHUNK, CHUNK), 0)
    col = lax.broadcasted_iota(I32, (CHUNK, CHUNK), 1)
    causal = row >= col
    tril = causal.astype(F32)
    triu = (row <= col).astype(F32)

    dtr = dtr_ref[0]
    dt = _softplus(dtr + dtb_row_ref[...])
    a_cum = jnp.dot(tril, dt * (-jnp.exp(alog_row_ref[...])), precision=HIGHEST,
                    preferred_element_type=F32)
    dt_t = _softplus(dtr.T + dtb_col_ref[...])
    a_cum_t = jnp.dot(dt_t * (-jnp.exp(alog_col_ref[...])), triu, precision=HIGHEST,
                      preferred_element_type=F32)
    w_state_t = dt_t * jnp.exp(a_cum_t[:, CHUNK - 1:CHUNK] - a_cum_t)
    exp_a = jnp.exp(a_cum)
    chunk_decay = jnp.dot(exp_a[CHUNK - SUBLANES:CHUNK, :], expand_ref[...], precision=HIGHEST,
                          preferred_element_type=F32)[SUBLANES - 1:SUBLANES, :]

    lane = lax.broadcasted_iota(I32, (CHUNK, LANES), 1)
    first_half = lane < HEAD_DIM

    def block_diag(v):
        zero = jnp.zeros_like(v)
        return jnp.concatenate([jnp.where(first_half, v, zero),
                                jnp.where(first_half, zero, v)], axis=0).astype(BF16)

    for g in range(N_GROUPS):
        b_g = xc[:, D_SSD + g * D_STATE:D_SSD + (g + 1) * D_STATE]
        c_g = xc[:, D_SSD + (N_GROUPS + g) * D_STATE:D_SSD + (N_GROUPS + g + 1) * D_STATE]
        cb = lax.dot_general(c_g.astype(BF16), b_g.astype(BF16), (((1,), (1,)), ((), ())),
                             preferred_element_type=F32)
        b_t = b_g.T
        for j in range(GROUP_W // LANES):
            lo = g * GROUP_W + j * LANES
            xp = xc[:, lo:lo + LANES]
            rhs_x = block_diag(xp)
            prev = state[g, :, j * LANES:(j + 1) * LANES]
            rhs_p = block_diag(prev)
            m_parts, o_parts, s_parts = [], [], []
            for hh in range(2):
                h = (lo // HEAD_DIM) + hh
                seg = a_cum[:, h:h + 1] - a_cum_t[h:h + 1, :]
                decay = jnp.exp(jnp.where(causal, seg, -jnp.inf))
                m_parts.append(cb * decay * dt_t[h:h + 1, :])
                o_parts.append(c_g * exp_a[:, h:h + 1])
                s_parts.append(b_t * w_state_t[h:h + 1, :])
            lhs_y = jnp.concatenate(m_parts, axis=1).astype(BF16)
            lhs_o = jnp.concatenate(o_parts, axis=1).astype(BF16)
            lhs_s = jnp.concatenate(s_parts, axis=1).astype(BF16)
            y_pair = (jnp.dot(lhs_y, rhs_x, preferred_element_type=F32)
                      + jnp.dot(lhs_o, rhs_p, preferred_element_type=F32))
            st_pair = jnp.dot(lhs_s, rhs_x, preferred_element_type=F32)
            state[g, :, j * LANES:(j + 1) * LANES] = (
                prev * chunk_decay[:, lo:lo + LANES] + st_pair)
            ybuf[:, lo:lo + LANES] = y_pair + dskip_ref[:, lo:lo + LANES] * xp

    zf = z_ref[0].astype(F32)
    yg = ybuf[...] * (zf * _sigmoid(zf))
    for g in range(N_GROUPS):
        blk = yg[:, g * GROUP_W:(g + 1) * GROUP_W]
        ms = jnp.mean(blk * blk, axis=-1, keepdims=True)
        y_ref[0, :, g * GROUP_W:(g + 1) * GROUP_W] = (
            blk * lax.rsqrt(ms + EPS) * nw_ref[:, g * GROUP_W:(g + 1) * GROUP_W]).astype(BF16)


def _ssd(xbc, z, dtr, conv_w, conv_b, dtb_row, alog_row, dtb_col, alog_col, dskip, norm_w, expand):
    tok = lambda n: pl.BlockSpec((1, CHUNK, n), lambda b, i: (b, i, 0))
    return pl.pallas_call(
        _ssd_kernel,
        grid=(NB, S // CHUNK),
        in_specs=[tok(D_XBC), tok(D_SSD), tok(LANES),
                  _resident((CONV_K, D_XBC)), _resident((1, D_XBC)),
                  _resident((1, LANES)), _resident((1, LANES)),
                  _resident((LANES, 1)), _resident((LANES, 1)),
                  _resident((1, D_SSD)), _resident((1, D_SSD)),
                  _resident((LANES, D_SSD))],
        out_specs=tok(D_SSD),
        out_shape=jax.ShapeDtypeStruct((NB, S, D_SSD), BF16),
        scratch_shapes=[pltpu.VMEM((SUBLANES + CHUNK, D_XBC), F32),
                        pltpu.VMEM((N_GROUPS, D_STATE, GROUP_W), F32),
                        pltpu.VMEM((CHUNK, D_SSD), F32)],
        compiler_params=pltpu.CompilerParams(
            dimension_semantics=("arbitrary", "arbitrary"), vmem_limit_bytes=VMEM_LIMIT),
        name="ssd",
    )(xbc, z, dtr, conv_w, conv_b, dtb_row, alog_row, dtb_col, alog_col, dskip, norm_w, expand)


def _merge_kernel(y_ref, up_ref, g_ref, x_ref, mod_ref, wso_ref, wpool_ref, pscale_ref,
                  wpo_ref, wout_ref, nw_ref, wr_ref, br_ref,
                  x1_ref, h2_ref, eidx_ref, grow_ref, ubuf):
    i = pl.program_id(1)
    tm = TM_MERGE

    @pl.when(i == 0)
    def _():
        ubuf[0:HALO, :] = jnp.zeros((HALO, D), F32)

    u = up_ref[0].astype(F32)
    ubuf[HALO:HALO + tm, :] = u
    ext = ubuf[...]
    ubuf[0:HALO, :] = ubuf[tm:tm + HALO, :]

    t_pos = i * tm + lax.broadcasted_iota(I32, (tm, 1), 0) + 1
    pooled = []
    for gi, w in enumerate(POOL_WINDOWS):
        acc = ext[:, gi * POOL_GW:(gi + 1) * POOL_GW]
        span = 1
        while span < w:
            acc = acc + pltpu.roll(acc, span, 0)
            span *= 2
        cnt = jnp.minimum(t_pos, w).astype(F32)
        p = acc[HALO:, :] / cnt - u[:, gi * POOL_GW:(gi + 1) * POOL_GW]
        pooled.append(jnp.dot(p.astype(BF16), wpool_ref[gi], preferred_element_type=F32))
    p = jnp.concatenate(pooled, axis=1) * pscale_ref[...]
    y_pool = jnp.dot(p.astype(BF16), wpo_ref[...], preferred_element_type=F32)
    y_ssd = jnp.dot(y_ref[0], wso_ref[...], preferred_element_type=F32)

    gt = _sigmoid(g_ref[0].astype(F32))
    merged = gt[:, :D] * y_ssd + gt[:, D:] * y_pool
    mix = jnp.dot(merged.astype(BF16), wout_ref[...], preferred_element_type=F32)
    x1 = x_ref[0] + mod_ref[0, 2:3, :] * mix
    x1_ref[0] = x1

    ms = jnp.mean(x1 * x1, axis=-1, keepdims=True)
    h2 = x1 * lax.rsqrt(ms + EPS) * nw_ref[...]
    h2 = h2 * (1.0 + mod_ref[0, 4:5, :]) + mod_ref[0, 3:4, :]
    h2_ref[0] = h2

    logits = lax.dot_general(wr_ref[...], h2, (((1,), (1,)), ((), ())), precision=HIGHEST,
                             preferred_element_type=F32) + br_ref[...]
    e_iota = lax.broadcasted_iota(I32, (N_EXP, tm), 0)
    vals, idxs = [], []
    for _ in range(TOP_K):
        m = jnp.max(logits, axis=0, keepdims=True)
        idx = jnp.min(jnp.where(logits == m, e_iota, N_EXP), axis=0, keepdims=True)
        vals.append(m)
        idxs.append(idx)
        logits = jnp.where(e_iota == idx, -jnp.inf, logits)
    exps = [jnp.exp(v - vals[0]) for v in vals]
    denom = exps[0] + exps[1] + exps[2] + exps[3]
    eidx_ref[0] = jnp.concatenate(idxs, axis=0)
    gates_t = jnp.concatenate([e / denom for e in exps]
                              + [jnp.zeros((LANES - TOP_K, tm), F32)], axis=0)
    grow_ref[0] = gates_t.T


def _merge(y, up, gts, x, mod, wso, wpool, pscale, wpo, wout, nw, wr_t, br_col):
    tm = TM_MERGE
    tok = lambda n: pl.BlockSpec((1, tm, n), lambda b, i: (b, i, 0))
    return pl.pallas_call(
        _merge_kernel,
        grid=(NB, S // tm),
        in_specs=[tok(D_SSD), tok(D), tok(2 * D), tok(D),
                  pl.BlockSpec((1, 6, D), lambda b, i: (b, 0, 0)),
                  _resident((D_SSD, D)), _resident((len(POOL_WINDOWS), POOL_GW, POOL_GW)),
                  _resident((1, D)), _resident((D, D)), _resident((D, D)),
                  _resident((1, D)), _resident((N_EXP, D)), _resident((N_EXP, 1))],
        out_specs=[tok(D), tok(D),
                   pl.BlockSpec((1, TOP_K, tm), lambda b, i: (b, 0, i)),
                   tok(LANES)],
        out_shape=[jax.ShapeDtypeStruct((NB, S, D), F32),
                   jax.ShapeDtypeStruct((NB, S, D), F32),
                   jax.ShapeDtypeStruct((NB, TOP_K, S), I32),
                   jax.ShapeDtypeStruct((NB, S, LANES), F32)],
        scratch_shapes=[pltpu.VMEM((HALO + tm, D), F32)],
        compiler_params=pltpu.CompilerParams(
            dimension_semantics=("arbitrary", "arbitrary"), vmem_limit_bytes=VMEM_LIMIT),
        name="merge",
    )(y, up, gts, x, mod, wso, wpool, pscale, wpo, wout, nw, wr_t, br_col)


def _sort_kernel(e_ref, sorted_ref, counts_ref):
    row = lax.broadcasted_iota(I32, (SORT_ROWS, LANES), 0)
    lane = lax.broadcasted_iota(I32, (SORT_ROWS, LANES), 1)
    flat = row * LANES + lane
    s_bits = S.bit_length() - 1
    slot = (((flat >> s_bits) & (TOP_K - 1)) * T
            + (flat >> (s_bits + 2)) * S + (flat & (S - 1)))
    e = e_ref[...]
    keys = (e << 16) | slot

    lane_e = lax.broadcasted_iota(I32, (SUBLANES, LANES), 1)
    counts = jnp.zeros((SUBLANES, LANES), I32)
    for ex in range(N_EXP):
        n = jnp.sum((e == ex).astype(F32))
        counts = jnp.where(lane_e == ex, n.astype(I32), counts)
    counts_ref[...] = counts

    def partner(v, d):
        if d < LANES:
            down = pltpu.roll(v, d, 1)
            up = pltpu.roll(v, LANES - d, 1)
            return jnp.where((lane & d) != 0, down, up)
        r = d // LANES
        down = pltpu.roll(v, r, 0)
        up = pltpu.roll(v, SORT_ROWS - r, 0)
        return jnp.where((row & r) != 0, down, up)

    k = 2
    while k <= N_SLOTS:
        d = k // 2
        while d >= 1:
            other = partner(keys, d)
            is_low = (flat & d) == 0
            ascending = (flat & k) == 0
            take_min = is_low == ascending
            keys = jnp.where(take_min, jnp.minimum(keys, other), jnp.maximum(keys, other))
            d //= 2
        k *= 2
    sorted_ref[...] = keys


def _sort(e2d):
    return pl.pallas_call(
        _sort_kernel,
        out_shape=[jax.ShapeDtypeStruct((SORT_ROWS, LANES), I32),
                   jax.ShapeDtypeStruct((SUBLANES, LANES), I32)],
        compiler_params=pltpu.CompilerParams(vmem_limit_bytes=VMEM_LIMIT),
        name="route_sort",
    )(e2d)


def _expert_kernel(sorted_ref, bexp_ref, bp0_ref, bend_ref, bfirst_ref, nused_ref,
                   h2_hbm, wgu_ref, bgu_ref, wd_ref, bd_ref, yg_hbm,
                   xbuf, obuf, wgu_bf, wd_bf, row_units, gsem, ssem):
    j = pl.program_id(0)
    n_used = nused_ref[0]

    def n_rows(jj):
        return jnp.clip(bend_ref[jj] - bp0_ref[jj], 0, BM)

    def row_slot(jj, i):
        return sorted_ref[bp0_ref[jj] + i] & (N_SLOTS - 1)

    def issue_gather(jj, buf):
        def body(i, carry):
            tok = row_slot(jj, i) & (T - 1)
            pltpu.make_async_copy(h2_hbm.at[pl.ds(tok, 1)], xbuf.at[buf, pl.ds(i, 1)],
                                  gsem.at[buf]).start()
            return carry
        lax.fori_loop(0, n_rows(jj), body, 0)

    def issue_scatter(jj, buf):
        def body(i, carry):
            pltpu.make_async_copy(obuf.at[buf, pl.ds(i, 1)],
                                  yg_hbm.at[pl.ds(row_slot(jj, i), 1)], ssem.at[buf]).start()
            return carry
        lax.fori_loop(0, n_rows(jj), body, 0)

    def wait_rows(jj, sem):
        n = n_rows(jj)
        pltpu.make_async_copy(row_units.at[pl.ds(0, n)], row_units.at[pl.ds(0, n)], sem).wait()

    def wait_gather(jj, buf):
        wait_rows(jj, gsem.at[buf])

    def wait_scatter(jj, buf):
        wait_rows(jj, ssem.at[buf])

    @pl.when(j == 0)
    def _():
        xbuf[...] = jnp.zeros_like(xbuf)
        issue_gather(0, 0)

    @pl.when(j < n_used)
    def _():
        buf = j % 2
        wait_gather(j, buf)

        @pl.when(j + 1 < n_used)
        def _():
            issue_gather(j + 1, 1 - buf)

        @pl.when(bfirst_ref[j] == 1)
        def _():
            wgu_bf[...] = wgu_ref[0].astype(BF16)
            wd_bf[...] = wd_ref[0].astype(BF16)

        x = xbuf[buf].astype(BF16)
        gu = jnp.dot(x, wgu_bf[...], preferred_element_type=F32) + bgu_ref[0]
        glu = jnp.minimum(gu[:, :D], SWIGLU_LIMIT)
        lin = jnp.clip(gu[:, D:], -SWIGLU_LIMIT, SWIGLU_LIMIT)
        act = glu * _sigmoid(SWIGLU_ALPHA * glu) * (lin + 1.0)
        y = jnp.dot(act.astype(BF16), wd_bf[...], preferred_element_type=F32) + bd_ref[0]

        @pl.when(j >= 2)
        def _():
            wait_scatter(j - 2, buf)

        obuf[buf] = y
        issue_scatter(j, buf)

        @pl.when(j == n_used - 1)
        def _():
            wait_scatter(j, buf)

            @pl.when(j >= 1)
            def _():
                wait_scatter(j - 1, 1 - buf)


def _experts(sorted_keys, bexp, bp0, bend, bfirst, nused, h2, w_gu, b_gu, w_down, b_down):
    grid_spec = pltpu.PrefetchScalarGridSpec(
        num_scalar_prefetch=6,
        grid=(N_BLOCKS,),
        in_specs=[pl.BlockSpec(memory_space=pl.ANY),
                  pl.BlockSpec((1, D, 2 * D), lambda j, srt, be, *_: (be[j], 0, 0)),
                  pl.BlockSpec((1, 1, 2 * D), lambda j, srt, be, *_: (be[j], 0, 0)),
                  pl.BlockSpec((1, D, D), lambda j, srt, be, *_: (be[j], 0, 0)),
                  pl.BlockSpec((1, 1, D), lambda j, srt, be, *_: (be[j], 0, 0))],
        out_specs=pl.BlockSpec(memory_space=pl.ANY),
        scratch_shapes=[pltpu.VMEM((2, BM, D), F32),
                        pltpu.VMEM((2, BM, D), F32),
                        pltpu.VMEM((D, 2 * D), BF16),
                        pltpu.VMEM((D, D), BF16),
                        pltpu.VMEM((BM, D // LANES, LANES), F32),
                        pltpu.SemaphoreType.DMA((2,)),
                        pltpu.SemaphoreType.DMA((2,))],
    )
    return pl.pallas_call(
        _expert_kernel,
        grid_spec=grid_spec,
        out_shape=jax.ShapeDtypeStruct((N_SLOTS, D), F32),
        compiler_params=pltpu.CompilerParams(
            dimension_semantics=("arbitrary",), vmem_limit_bytes=VMEM_LIMIT),
        name="experts",
    )(sorted_keys, bexp, bp0, bend, bfirst, nused, h2, w_gu, b_gu, w_down, b_down)


def _final_kernel(y0_ref, y1_ref, y2_ref, y3_ref, grow_ref, x1_ref, mod_ref, nw_ref, o_ref):
    gr = grow_ref[0]
    y = (gr[:, 0:1] * y0_ref[...] + gr[:, 1:2] * y1_ref[...]
         + gr[:, 2:3] * y2_ref[...] + gr[:, 3:4] * y3_ref[...])
    x2 = x1_ref[0] + mod_ref[0, 5:6, :] * y
    ms = jnp.mean(x2 * x2, axis=-1, keepdims=True)
    o_ref[0] = x2 * lax.rsqrt(ms + EPS) * nw_ref[...]


def _final(yg, grow, x1, mod, nw):
    tm = TM_FINAL
    per_b = S // tm
    per_k = T // tm

    def yspec(k):
        return pl.BlockSpec((tm, D), lambda b, i: (k * per_k + b * per_b + i, 0))

    tok = lambda n: pl.BlockSpec((1, tm, n), lambda b, i: (b, i, 0))
    return pl.pallas_call(
        _final_kernel,
        grid=(NB, per_b),
        in_specs=[yspec(0), yspec(1), yspec(2), yspec(3), tok(LANES), tok(D),
                  pl.BlockSpec((1, 6, D), lambda b, i: (b, 0, 0)),
                  _resident((1, D))],
        out_specs=tok(D),
        out_shape=jax.ShapeDtypeStruct((NB, S, D), F32),
        compiler_params=pltpu.CompilerParams(
            dimension_semantics=("arbitrary", "arbitrary"), vmem_limit_bytes=VMEM_LIMIT),
        name="final",
    )(yg, yg, yg, yg, grow, x1, mod, nw)


def _block_tables(counts):
    nblk = (counts + BM - 1) // BM
    blk_cum = jnp.cumsum(nblk)
    n_used = blk_cum[-1]
    blk_start = blk_cum - nblk
    start = jnp.cumsum(counts) - counts
    jj = jnp.arange(N_BLOCKS, dtype=I32)
    e_j = jnp.searchsorted(blk_cum, jnp.minimum(jj, n_used - 1), side="right").astype(I32)
    e_j = jnp.minimum(e_j, N_EXP - 1)
    bp0 = start[e_j] + (jj - blk_start[e_j]) * BM
    bend = start[e_j] + counts[e_j]
    bfirst = (jj == blk_start[e_j]).astype(I32)
    return e_j, bp0.astype(I32), bend.astype(I32), bfirst, n_used.reshape(1).astype(I32)


def kernel(x, c, w_ada, b_ada, norm_mix_w, w_in, conv_w, conv_b, dt_bias, a_log, d_skip,
           ssd_norm_w, w_ssd_out, w_pool, pool_scale, w_pool_out, w_out, norm_ffn_w,
           w_router, b_router, w_gu, b_gu, w_down, b_down, norm_final_w):
    assert x.shape == (NB, S, D) and w_ada.shape[0] == 1

    mod = _adaln(c, w_ada[0], b_ada).reshape(NB, 6, D)

    w_in_b = w_in[0].astype(BF16)
    o1, o2, o3, o4 = D_SSD, D_SSD + D_XBC, D_SSD + D_XBC + N_HEADS, D_SSD + D_XBC + N_HEADS + D
    wz, wx, wp, wg = w_in_b[:, :o1], w_in_b[:, o1:o2], w_in_b[:, o3:o4], w_in_b[:, o4:]
    wdt = jnp.pad(w_in_b[:, o2:o3], ((0, 0), (0, LANES - N_HEADS)))
    z, xbc, dtr, up, gts = _inproj(x, mod, norm_mix_w, wz, wx, wdt, wp, wg)

    pad_h = lambda v: jnp.pad(v[0], (0, LANES - N_HEADS))
    dtb, alog = pad_h(dt_bias), pad_h(a_log)
    head_of_chan = jnp.arange(D_SSD, dtype=I32) // HEAD_DIM
    expand = (jnp.arange(LANES, dtype=I32)[:, None] == head_of_chan[None, :]).astype(F32)
    dskip = jnp.repeat(d_skip[0], HEAD_DIM).reshape(1, D_SSD)
    y = _ssd(xbc, z, dtr, conv_w[0], conv_b, dtb.reshape(1, LANES), alog.reshape(1, LANES),
             dtb.reshape(LANES, 1), alog.reshape(LANES, 1), dskip, ssd_norm_w, expand)

    x1, h2, eidx, grow = _merge(
        y, up, gts, x, mod, w_ssd_out[0].astype(BF16), w_pool[0].astype(BF16), pool_scale,
        w_pool_out[0].astype(BF16), w_out[0].astype(BF16), norm_ffn_w,
        w_router[0].T, b_router.reshape(N_EXP, 1))

    sorted_keys, counts = _sort(eidx.reshape(SORT_ROWS, LANES))
    bexp, bp0, bend, bfirst, nused = _block_tables(counts[0, :N_EXP])

    yg = _experts(sorted_keys.reshape(N_SLOTS), bexp, bp0, bend, bfirst, nused,
                  h2.reshape(T, D), w_gu[0], b_gu[0].reshape(N_EXP, 1, 2 * D),
                  w_down[0], b_down[0].reshape(N_EXP, 1, D))

    return _final(yg, grow, x1, mod, norm_final_w.reshape(1, D))
```

```python
import functools

import jax
import jax.numpy as jnp
from jax import lax
from jax.experimental import pallas as pl
from jax.experimental.pallas import tpu as pltpu

F32 = jnp.float32
BF16 = jnp.bfloat16
I32 = jnp.int32
HIGHEST = lax.Precision.HIGHEST

D = 1024
NB = 4
S = 4096
T = NB * S
EPS = 1e-5
D_SSD = 2048
N_HEADS = 32
HEAD_DIM = 64
N_GROUPS = 4
D_STATE = 128
GROUP_W = D_SSD // N_GROUPS
D_XBC = D_SSD + 2 * N_GROUPS * D_STATE
CONV_K = 4
POOL_WINDOWS = (2, 4, 8, 16)
POOL_GW = 256
N_EXP = 32
TOP_K = 4
SWIGLU_LIMIT = 7.0
SWIGLU_ALPHA = 1.702

LANES = 128
SUBLANES = 8

CHUNK = 128
TM_PROJ = 512
TM_MERGE = 512
TM_FINAL = 512
BM = 128
N_SLOTS = T * TOP_K
N_BLOCKS = N_SLOTS // BM + N_EXP
HALO = 16
SORT_ROWS = N_SLOTS // LANES

VMEM_LIMIT = 56 * 1024 * 1024


def _sigmoid(v):
    return 1.0 / (1.0 + jnp.exp(-v))


def _softplus(v):
    return jnp.maximum(v, 0.0) + jnp.log(1.0 + jnp.exp(-jnp.abs(v)))


def _resident(shape):
    nd = len(shape)
    return pl.BlockSpec(shape, lambda *_: (0,) * nd, pipeline_mode=pl.Buffered(1))


def _adaln_kernel(c_ref, w_ref, b_ref, o_ref):
    c = c_ref[...]
    act = c * _sigmoid(c)
    o_ref[...] = jnp.dot(act, w_ref[...], precision=HIGHEST,
                         preferred_element_type=F32) + b_ref[...]


def _adaln(c, w_ada, b_ada):
    tn = 1536
    return pl.pallas_call(
        _adaln_kernel,
        grid=(6 * D // tn,),
        in_specs=[pl.BlockSpec((NB, D), lambda j: (0, 0)),
                  pl.BlockSpec((D, tn), lambda j: (0, j)),
                  pl.BlockSpec((1, tn), lambda j: (0, j))],
        out_specs=pl.BlockSpec((NB, tn), lambda j: (0, j)),
        out_shape=jax.ShapeDtypeStruct((NB, 6 * D), F32),
        compiler_params=pltpu.CompilerParams(vmem_limit_bytes=VMEM_LIMIT),
        name="adaln",
    )(c, w_ada, b_ada)


def _inproj_kernel(x_ref, mod_ref, nw_ref, wz_ref, wx_ref, wdt_ref, wp_ref, wg_ref,
                   z_ref, xbc_ref, dt_ref, up_ref, g_ref):
    x = x_ref[0]
    ms = jnp.mean(x * x, axis=-1, keepdims=True)
    h = x * lax.rsqrt(ms + EPS) * nw_ref[...]
    h = h * (1.0 + mod_ref[0, 1:2, :]) + mod_ref[0, 0:1, :]
    hb = h.astype(BF16)
    z_ref[0] = jnp.dot(hb, wz_ref[...], preferred_element_type=F32).astype(BF16)
    xbc_ref[0] = jnp.dot(hb, wx_ref[...], preferred_element_type=F32).astype(BF16)
    dt_ref[0] = jnp.dot(hb, wdt_ref[...], preferred_element_type=F32)
    up_ref[0] = jnp.dot(hb, wp_ref[...], preferred_element_type=F32).astype(BF16)
    g_ref[0] = jnp.dot(hb, wg_ref[...], preferred_element_type=F32).astype(BF16)


def _inproj(x, mod, norm_w, wz, wx, wdt, wp, wg):
    tm = TM_PROJ
    tok = lambda n: pl.BlockSpec((1, tm, n), lambda b, i: (b, i, 0))
    return pl.pallas_call(
        _inproj_kernel,
        grid=(NB, S // tm),
        in_specs=[tok(D),
                  pl.BlockSpec((1, 6, D), lambda b, i: (b, 0, 0)),
                  _resident((1, D)),
                  _resident((D, D_SSD)), _resident((D, D_XBC)), _resident((D, LANES)),
                  _resident((D, D)), _resident((D, 2 * D))],
        out_specs=[tok(D_SSD), tok(D_XBC), tok(LANES), tok(D), tok(2 * D)],
        out_shape=[jax.ShapeDtypeStruct((NB, S, D_SSD), BF16),
                   jax.ShapeDtypeStruct((NB, S, D_XBC), BF16),
                   jax.ShapeDtypeStruct((NB, S, LANES), F32),
                   jax.ShapeDtypeStruct((NB, S, D), BF16),
                   jax.ShapeDtypeStruct((NB, S, 2 * D), BF16)],
        compiler_params=pltpu.CompilerParams(
            dimension_semantics=("arbitrary", "arbitrary"), vmem_limit_bytes=VMEM_LIMIT),
        name="inproj",
    )(x, mod, norm_w, wz, wx, wdt, wp, wg)


def _ssd_kernel(xbc_ref, z_ref, dtr_ref, cw_ref, cb_ref, dtb_row_ref, alog_row_ref,
                dtb_col_ref, alog_col_ref, dskip_ref, nw_ref, expand_ref,
                y_ref, ubuf, state, ybuf):
    c = pl.program_id(1)

    @pl.when(c == 0)
    def _():
        ubuf[0:SUBLANES, :] = jnp.zeros((SUBLANES, D_XBC), F32)
        state[...] = jnp.zeros_like(state)

    ubuf[SUBLANES:SUBLANES + CHUNK, :] = xbc_ref[0].astype(F32)
    conv = cb_ref[...] + cw_ref[0:1, :] * ubuf[pl.ds(SUBLANES - 3, CHUNK), :]
    for k in range(1, CONV_K):
        conv = conv + cw_ref[k:k + 1, :] * ubuf[pl.ds(SUBLANES - 3 + k, CHUNK), :]
    ubuf[0:SUBLANES, :] = ubuf[CHUNK:CHUNK + SUBLANES, :]
    xc = conv * _sigmoid(conv)

    row = lax.broadcasted_iota(I32, (CHUNK, CHUNK), 0)
    col = lax.broadcasted_iota(I32, (CHUNK, CHUNK), 1)
    causal = row >= col
    tril = causal.astype(F32)
    triu = (row <= col).astype(F32)

    dtr = dtr_ref[0]
    dt = _softplus(dtr + dtb_row_ref[...])
    a_cum = jnp.dot(tril, dt * (-jnp.exp(alog_row_ref[...])), precision=HIGHEST,
                    preferred_element_type=F32)
    dt_t = _softplus(dtr.T + dtb_col_ref[...])
    a_cum_t = jnp.dot(dt_t * (-jnp.exp(alog_col_ref[...])), triu, precision=HIGHEST,
                      preferred_element_type=F32)
    w_state_t = dt_t * jnp.exp(a_cum_t[:, CHUNK - 1:CHUNK] - a_cum_t)
    exp_a = jnp.exp(a_cum)
    chunk_decay = jnp.dot(exp_a[CHUNK - SUBLANES:CHUNK, :], expand_ref[...], precision=HIGHEST,
                          preferred_element_type=F32)[SUBLANES - 1:SUBLANES, :]

    lane = lax.broadcasted_iota(I32, (CHUNK, LANES), 1)
    first_half = lane < HEAD_DIM

    def block_diag(v):
        zero = jnp.zeros_like(v)
        return jnp.concatenate([jnp.where(first_half, v, zero),
                                jnp.where(first_half, zero, v)], axis=0).astype(BF16)

    for g in range(N_GROUPS):
        b_g = xc[:, D_SSD + g * D_STATE:D_SSD + (g + 1) * D_STATE]
        c_g = xc[:, D_SSD + (N_GROUPS + g) * D_STATE:D_SSD + (N_GROUPS + g + 1) * D_STATE]
        cb = lax.dot_general(c_g.astype(BF16), b_g.astype(BF16), (((1,), (1,)), ((), ())),
                             preferred_element_type=F32)
        b_t = b_g.T
        for j in range(GROUP_W // LANES):
            lo = g * GROUP_W + j * LANES
            xp = xc[:, lo:lo + LANES]
            rhs_x = block_diag(xp)
            prev = state[g, :, j * LANES:(j + 1) * LANES]
            rhs_p = block_diag(prev)
            m_parts, o_parts, s_parts = [], [], []
            for hh in range(2):
                h = (lo // HEAD_DIM) + hh
                seg = a_cum[:, h:h + 1] - a_cum_t[h:h + 1, :]
                decay = jnp.exp(jnp.where(causal, seg, -jnp.inf))
                m_parts.append(cb * decay * dt_t[h:h + 1, :])
                o_parts.append(c_g * exp_a[:, h:h + 1])
                s_parts.append(b_t * w_state_t[h:h + 1, :])
            lhs_y = jnp.concatenate(m_parts, axis=1).astype(BF16)
            lhs_o = jnp.concatenate(o_parts, axis=1).astype(BF16)
            lhs_s = jnp.concatenate(s_parts, axis=1).astype(BF16)
            y_pair = (jnp.dot(lhs_y, rhs_x, preferred_element_type=F32)
                      + jnp.dot(lhs_o, rhs_p, preferred_element_type=F32))
            st_pair = jnp.dot(lhs_s, rhs_x, preferred_element_type=F32)
            state[g, :, j * LANES:(j + 1) * LANES] = (
                prev * chunk_decay[:, lo:lo + LANES] + st_pair)
            ybuf[:, lo:lo + LANES] = y_pair + dskip_ref[:, lo:lo + LANES] * xp

    zf = z_ref[0].astype(F32)
    yg = ybuf[...] * (zf * _sigmoid(zf))
    for g in range(N_GROUPS):
        blk = yg[:, g * GROUP_W:(g + 1) * GROUP_W]
        ms = jnp.mean(blk * blk, axis=-1, keepdims=True)
        y_ref[0, :, g * GROUP_W:(g + 1) * GROUP_W] = (
            blk * lax.rsqrt(ms + EPS) * nw_ref[:, g * GROUP_W:(g + 1) * GROUP_W]).astype(BF16)


def _ssd(xbc, z, dtr, conv_w, conv_b, dtb_row, alog_row, dtb_col, alog_col, dskip, norm_w, expand):
    tok = lambda n: pl.BlockSpec((1, CHUNK, n), lambda b, i: (b, i, 0))
    return pl.pallas_call(
        _ssd_kernel,
        grid=(NB, S // CHUNK),
        in_specs=[tok(D_XBC), tok(D_SSD), tok(LANES),
                  _resident((CONV_K, D_XBC)), _resident((1, D_XBC)),
                  _resident((1, LANES)), _resident((1, LANES)),
                  _resident((LANES, 1)), _resident((LANES, 1)),
                  _resident((1, D_SSD)), _resident((1, D_SSD)),
                  _resident((LANES, D_SSD))],
        out_specs=tok(D_SSD),
        out_shape=jax.ShapeDtypeStruct((NB, S, D_SSD), BF16),
        scratch_shapes=[pltpu.VMEM((SUBLANES + CHUNK, D_XBC), F32),
                        pltpu.VMEM((N_GROUPS, D_STATE, GROUP_W), F32),
                        pltpu.VMEM((CHUNK, D_SSD), F32)],
        compiler_params=pltpu.CompilerParams(
            dimension_semantics=("arbitrary", "arbitrary"), vmem_limit_bytes=VMEM_LIMIT),
        name="ssd",
    )(xbc, z, dtr, conv_w, conv_b, dtb_row, alog_row, dtb_col, alog_col, dskip, norm_w, expand)


def _merge_kernel(y_ref, up_ref, g_ref, x_ref, mod_ref, wso_ref, wpool_ref, pscale_ref,
                  wpo_ref, wout_ref, nw_ref, wr_ref, br_ref,
                  x1_ref, h2_ref, eidx_ref, grow_ref, ubuf):
    i = pl.program_id(1)
    tm = TM_MERGE

    @pl.when(i == 0)
    def _():
        ubuf[0:HALO, :] = jnp.zeros((HALO, D), F32)

    u = up_ref[0].astype(F32)
    ubuf[HALO:HALO + tm, :] = u
    ext = ubuf[...]
    ubuf[0:HALO, :] = ubuf[tm:tm + HALO, :]

    t_pos = i * tm + lax.broadcasted_iota(I32, (tm, 1), 0) + 1
    pooled = []
    for gi, w in enumerate(POOL_WINDOWS):
        acc = ext[:, gi * POOL_GW:(gi + 1) * POOL_GW]
        span = 1
        while span < w:
            acc = acc + pltpu.roll(acc, span, 0)
            span *= 2
        cnt = jnp.minimum(t_pos, w).astype(F32)
        p = acc[HALO:, :] / cnt - u[:, gi * POOL_GW:(gi + 1) * POOL_GW]
        pooled.append(jnp.dot(p.astype(BF16), wpool_ref[gi], preferred_element_type=F32))
    p = jnp.concatenate(pooled, axis=1) * pscale_ref[...]
    y_pool = jnp.dot(p.astype(BF16), wpo_ref[...], preferred_element_type=F32)
    y_ssd = jnp.dot(y_ref[0], wso_ref[...], preferred_element_type=F32)

    gt = _sigmoid(g_ref[0].astype(F32))
    merged = gt[:, :D] * y_ssd + gt[:, D:] * y_pool
    mix = jnp.dot(merged.astype(BF16), wout_ref[...], preferred_element_type=F32)
    x1 = x_ref[0] + mod_ref[0, 2:3, :] * mix
    x1_ref[0] = x1

    ms = jnp.mean(x1 * x1, axis=-1, keepdims=True)
    h2 = x1 * lax.rsqrt(ms + EPS) * nw_ref[...]
    h2 = h2 * (1.0 + mod_ref[0, 4:5, :]) + mod_ref[0, 3:4, :]
    h2_ref[0] = h2

    logits = lax.dot_general(wr_ref[...], h2, (((1,), (1,)), ((), ())), precision=HIGHEST,
                             preferred_element_type=F32) + br_ref[...]
    e_iota = lax.broadcasted_iota(I32, (N_EXP, tm), 0)
    vals, idxs = [], []
    for _ in range(TOP_K):
        m = jnp.max(logits, axis=0, keepdims=True)
        idx = jnp.min(jnp.where(logits == m, e_iota, N_EXP), axis=0, keepdims=True)
        vals.append(m)
        idxs.append(idx)
        logits = jnp.where(e_iota == idx, -jnp.inf, logits)
    exps = [jnp.exp(v - vals[0]) for v in vals]
    denom = exps[0] + exps[1] + exps[2] + exps[3]
    eidx_ref[0] = jnp.concatenate(idxs, axis=0)
    gates_t = jnp.concatenate([e / denom for e in exps]
                              + [jnp.zeros((LANES - TOP_K, tm), F32)], axis=0)
    grow_ref[0] = gates_t.T


def _merge(y, up, gts, x, mod, wso, wpool, pscale, wpo, wout, nw, wr_t, br_col):
    tm = TM_MERGE
    tok = lambda n: pl.BlockSpec((1, tm, n), lambda b, i: (b, i, 0))
    return pl.pallas_call(
        _merge_kernel,
        grid=(NB, S // tm),
        in_specs=[tok(D_SSD), tok(D), tok(2 * D), tok(D),
                  pl.BlockSpec((1, 6, D), lambda b, i: (b, 0, 0)),
                  _resident((D_SSD, D)), _resident((len(POOL_WINDOWS), POOL_GW, POOL_GW)),
                  _resident((1, D)), _resident((D, D)), _resident((D, D)),
                  _resident((1, D)), _resident((N_EXP, D)), _resident((N_EXP, 1))],
        out_specs=[tok(D), tok(D),
                   pl.BlockSpec((1, TOP_K, tm), lambda b, i: (b, 0, i)),
                   tok(LANES)],
        out_shape=[jax.ShapeDtypeStruct((NB, S, D), F32),
                   jax.ShapeDtypeStruct((NB, S, D), F32),
                   jax.ShapeDtypeStruct((NB, TOP_K, S), I32),
                   jax.ShapeDtypeStruct((NB, S, LANES), F32)],
        scratch_shapes=[pltpu.VMEM((HALO + tm, D), F32)],
        compiler_params=pltpu.CompilerParams(
            dimension_semantics=("arbitrary", "arbitrary"), vmem_limit_bytes=VMEM_LIMIT),
        name="merge",
    )(y, up, gts, x, mod, wso, wpool, pscale, wpo, wout, nw, wr_t, br_col)


def _sort_kernel(e_ref, sorted_ref, counts_ref):
    row = lax.broadcasted_iota(I32, (SORT_ROWS, LANES), 0)
    lane = lax.broadcasted_iota(I32, (SORT_ROWS, LANES), 1)
    flat = row * LANES + lane
    s_bits = S.bit_length() - 1
    slot = (((flat >> s_bits) & (TOP_K - 1)) * T
            + (flat >> (s_bits + 2)) * S + (flat & (S - 1)))
    e = e_ref[...]
    keys = (e << 16) | slot

    lane_e = lax.broadcasted_iota(I32, (SUBLANES, LANES), 1)
    counts = jnp.zeros((SUBLANES, LANES), I32)
    for ex in range(N_EXP):
        n = jnp.sum((e == ex).astype(F32))
        counts = jnp.where(lane_e == ex, n.astype(I32), counts)
    counts_ref[...] = counts

    def partner(v, d):
        if d < LANES:
            down = pltpu.roll(v, d, 1)
            up = pltpu.roll(v, LANES - d, 1)
            return jnp.where((lane & d) != 0, down, up)
        r = d // LANES
        down = pltpu.roll(v, r, 0)
        up = pltpu.roll(v, SORT_ROWS - r, 0)
        return jnp.where((row & r) != 0, down, up)

    k = 2
    while k <= N_SLOTS:
        d = k // 2
        while d >= 1:
            other = partner(keys, d)
            is_low = (flat & d) == 0
            ascending = (flat & k) == 0
            take_min = is_low == ascending
            keys = jnp.where(take_min, jnp.minimum(keys, other), jnp.maximum(keys, other))
            d //= 2
        k *= 2
    sorted_ref[...] = keys


def _sort(e2d):
    return pl.pallas_call(
        _sort_kernel,
        out_shape=[jax.ShapeDtypeStruct((SORT_ROWS, LANES), I32),
                   jax.ShapeDtypeStruct((SUBLANES, LANES), I32)],
        compiler_params=pltpu.CompilerParams(vmem_limit_bytes=VMEM_LIMIT),
        name="route_sort",
    )(e2d)


def _expert_kernel(sorted_ref, bexp_ref, bp0_ref, bend_ref, bfirst_ref, nused_ref,
                   h2_hbm, wgu_ref, bgu_ref, wd_ref, bd_ref, yg_hbm,
                   xbuf, obuf, wgu_bf, wd_bf, row_units, gsem, ssem):
    j = pl.program_id(0)
    n_used = nused_ref[0]

    def n_rows(jj):
        return jnp.clip(bend_ref[jj] - bp0_ref[jj], 0, BM)

    def row_slot(jj, i):
        return sorted_ref[bp0_ref[jj] + i] & (N_SLOTS - 1)

    def issue_gather(jj, buf):
        def body(i, carry):
            tok = row_slot(jj, i) & (T - 1)
            pltpu.make_async_copy(h2_hbm.at[pl.ds(tok, 1)], xbuf.at[buf, pl.ds(i, 1)],
                                  gsem.at[buf]).start()
            return carry
        lax.fori_loop(0, n_rows(jj), body, 0)

    def issue_scatter(jj, buf):
        def body(i, carry):
            pltpu.make_async_copy(obuf.at[buf, pl.ds(i, 1)],
                                  yg_hbm.at[pl.ds(row_slot(jj, i), 1)], ssem.at[buf]).start()
            return carry
        lax.fori_loop(0, n_rows(jj), body, 0)

    def wait_rows(jj, sem):
        n = n_rows(jj)
        pltpu.make_async_copy(row_units.at[pl.ds(0, n)], row_units.at[pl.ds(0, n)], sem).wait()

    def wait_gather(jj, buf):
        wait_rows(jj, gsem.at[buf])

    def wait_scatter(jj, buf):
        wait_rows(jj, ssem.at[buf])

    @pl.when(j == 0)
    def _():
        xbuf[...] = jnp.zeros_like(xbuf)
        issue_gather(0, 0)

    @pl.when(j < n_used)
    def _():
        buf = j % 2
        wait_gather(j, buf)

        @pl.when(j + 1 < n_used)
        def _():
            issue_gather(j + 1, 1 - buf)

        @pl.when(bfirst_ref[j] == 1)
        def _():
            wgu_bf[...] = wgu_ref[0].astype(BF16)
            wd_bf[...] = wd_ref[0].astype(BF16)

        x = xbuf[buf].astype(BF16)
        gu = jnp.dot(x, wgu_bf[...], preferred_element_type=F32) + bgu_ref[0]
        glu = jnp.minimum(gu[:, :D], SWIGLU_LIMIT)
        lin = jnp.clip(gu[:, D:], -SWIGLU_LIMIT, SWIGLU_LIMIT)
        act = glu * _sigmoid(SWIGLU_ALPHA * glu) * (lin + 1.0)
        y = jnp.dot(act.astype(BF16), wd_bf[...], preferred_element_type=F32) + bd_ref[0]

        @pl.when(j >= 2)
        def _():
            wait_scatter(j - 2, buf)

        obuf[buf] = y
        issue_scatter(j, buf)

        @pl.when(j == n_used - 1)
        def _():
            wait_scatter(j, buf)

            @pl.when(j >= 1)
            def _():
                wait_scatter(j - 1, 1 - buf)


def _experts(sorted_keys, bexp, bp0, bend, bfirst, nused, h2, w_gu, b_gu, w_down, b_down):
    grid_spec = pltpu.PrefetchScalarGridSpec(
        num_scalar_prefetch=6,
        grid=(N_BLOCKS,),
        in_specs=[pl.BlockSpec(memory_space=pl.ANY),
                  pl.BlockSpec((1, D, 2 * D), lambda j, srt, be, *_: (be[j], 0, 0)),
                  pl.BlockSpec((1, 1, 2 * D), lambda j, srt, be, *_: (be[j], 0, 0)),
                  pl.BlockSpec((1, D, D), lambda j, srt, be, *_: (be[j], 0, 0)),
                  pl.BlockSpec((1, 1, D), lambda j, srt, be, *_: (be[j], 0, 0))],
        out_specs=pl.BlockSpec(memory_space=pl.ANY),
        scratch_shapes=[pltpu.VMEM((2, BM, D), F32),
                        pltpu.VMEM((2, BM, D), F32),
                        pltpu.VMEM((D, 2 * D), BF16),
                        pltpu.VMEM((D, D), BF16),
                        pltpu.VMEM((BM, D // LANES, LANES), F32),
                        pltpu.SemaphoreType.DMA((2,)),
                        pltpu.SemaphoreType.DMA((2,))],
    )
    return pl.pallas_call(
        _expert_kernel,
        grid_spec=grid_spec,
        out_shape=jax.ShapeDtypeStruct((N_SLOTS, D), F32),
        compiler_params=pltpu.CompilerParams(
            dimension_semantics=("arbitrary",), vmem_limit_bytes=VMEM_LIMIT),
        name="experts",
    )(sorted_keys, bexp, bp0, bend, bfirst, nused, h2, w_gu, b_gu, w_down, b_down)


def _final_kernel(y0_ref, y1_ref, y2_ref, y3_ref, grow_ref, x1_ref, mod_ref, nw_ref, o_ref):
    gr = grow_ref[0]
    y = (gr[:, 0:1] * y0_ref[...] + gr[:, 1:2] * y1_ref[...]
         + gr[:, 2:3] * y2_ref[...] + gr[:, 3:4] * y3_ref[...])
    x2 = x1_ref[0] + mod_ref[0, 5:6, :] * y
    ms = jnp.mean(x2 * x2, axis=-1, keepdims=True)
    o_ref[0] = x2 * lax.rsqrt(ms + EPS) * nw_ref[...]


def _final(yg, grow, x1, mod, nw):
    tm = TM_FINAL
    per_b = S // tm
    per_k = T // tm

    def yspec(k):
        return pl.BlockSpec((tm, D), lambda b, i: (k * per_k + b * per_b + i, 0))

    tok = lambda n: pl.BlockSpec((1, tm, n), lambda b, i: (b, i, 0))
    return pl.pallas_call(
        _final_kernel,
        grid=(NB, per_b),
        in_specs=[yspec(0), yspec(1), yspec(2), yspec(3), tok(LANES), tok(D),
                  pl.BlockSpec((1, 6, D), lambda b, i: (b, 0, 0)),
                  _resident((1, D))],
        out_specs=tok(D),
        out_shape=jax.ShapeDtypeStruct((NB, S, D), F32),
        compiler_params=pltpu.CompilerParams(
            dimension_semantics=("arbitrary", "arbitrary"), vmem_limit_bytes=VMEM_LIMIT),
        name="final",
    )(yg, yg, yg, yg, grow, x1, mod, nw)


def _block_tables(counts):
    nblk = (counts + BM - 1) // BM
    blk_cum = jnp.cumsum(nblk)
    n_used = blk_cum[-1]
    blk_start = blk_cum - nblk
    start = jnp.cumsum(counts) - counts
    jj = jnp.arange(N_BLOCKS, dtype=I32)
    jc = jnp.minimum(jj, n_used - 1)[:, None]
    owner = ((blk_start[None, :] <= jc) & (jc < blk_cum[None, :])).astype(I32)
    pick = lambda v: jnp.sum(owner * v[None, :], axis=1)
    e_j = pick(jnp.arange(N_EXP, dtype=I32))
    bp0 = pick(start) + (jj - pick(blk_start)) * BM
    bend = pick(start + counts)
    bfirst = (jj == pick(blk_start)).astype(I32)
    return e_j, bp0.astype(I32), bend.astype(I32), bfirst, n_used.reshape(1).astype(I32)


def kernel(x, c, w_ada, b_ada, norm_mix_w, w_in, conv_w, conv_b, dt_bias, a_log, d_skip,
           ssd_norm_w, w_ssd_out, w_pool, pool_scale, w_pool_out, w_out, norm_ffn_w,
           w_router, b_router, w_gu, b_gu, w_down, b_down, norm_final_w):
    assert x.shape == (NB, S, D) and w_ada.shape[0] == 1

    mod = _adaln(c, w_ada[0], b_ada).reshape(NB, 6, D)

    w_in_b = w_in[0].astype(BF16)
    o1, o2, o3, o4 = D_SSD, D_SSD + D_XBC, D_SSD + D_XBC + N_HEADS, D_SSD + D_XBC + N_HEADS + D
    wz, wx, wp, wg = w_in_b[:, :o1], w_in_b[:, o1:o2], w_in_b[:, o3:o4], w_in_b[:, o4:]
    wdt = jnp.pad(w_in_b[:, o2:o3], ((0, 0), (0, LANES - N_HEADS)))
    z, xbc, dtr, up, gts = _inproj(x, mod, norm_mix_w, wz, wx, wdt, wp, wg)

    pad_h = lambda v: jnp.pad(v[0], (0, LANES - N_HEADS))
    dtb, alog = pad_h(dt_bias), pad_h(a_log)
    head_of_chan = jnp.arange(D_SSD, dtype=I32) // HEAD_DIM
    expand = (jnp.arange(LANES, dtype=I32)[:, None] == head_of_chan[None, :]).astype(F32)
    dskip = jnp.repeat(d_skip[0], HEAD_DIM).reshape(1, D_SSD)
    y = _ssd(xbc, z, dtr, conv_w[0], conv_b, dtb.reshape(1, LANES), alog.reshape(1, LANES),
             dtb.reshape(LANES, 1), alog.reshape(LANES, 1), dskip, ssd_norm_w, expand)

    x1, h2, eidx, grow = _merge(
        y, up, gts, x, mod, w_ssd_out[0].astype(BF16), w_pool[0].astype(BF16), pool_scale,
        w_pool_out[0].astype(BF16), w_out[0].astype(BF16), norm_ffn_w,
        w_router[0].T, b_router.reshape(N_EXP, 1))

    sorted_keys, counts = _sort(eidx.reshape(SORT_ROWS, LANES))
    bexp, bp0, bend, bfirst, nused = _block_tables(counts[0, :N_EXP])

    yg = _experts(sorted_keys.reshape(N_SLOTS), bexp, bp0, bend, bfirst, nused,
                  h2.reshape(T, D), w_gu[0], b_gu[0].reshape(N_EXP, 1, 2 * D),
                  w_down[0], b_down[0].reshape(N_EXP, 1, D))

    return _final(yg, grow, x1, mod, norm_final_w.reshape(1, D))
```

```python
import functools

import jax
import jax.numpy as jnp
from jax import lax
from jax.experimental import pallas as pl
from jax.experimental.pallas import tpu as pltpu
from jax.experimental.pallas import tpu_sc as plsc

F32 = jnp.float32
BF16 = jnp.bfloat16
I32 = jnp.int32
HIGHEST = lax.Precision.HIGHEST

D = 1024
NB = 4
S = 4096
T = NB * S
EPS = 1e-5
D_SSD = 2048
N_HEADS = 32
HEAD_DIM = 64
N_GROUPS = 4
D_STATE = 128
GROUP_W = D_SSD // N_GROUPS
D_XBC = D_SSD + 2 * N_GROUPS * D_STATE
CONV_K = 4
POOL_WINDOWS = (2, 4, 8, 16)
POOL_GW = 256
N_EXP = 32
TOP_K = 4
SWIGLU_LIMIT = 7.0
SWIGLU_ALPHA = 1.702

LANES = 128
SUBLANES = 8

CHUNK = 128
TM_PROJ = 512
TM_MERGE = 512
TM_FINAL = 512
BM = 128
N_SLOTS = T * TOP_K
N_BLOCKS = N_SLOTS // BM + N_EXP
HALO = 16
SORT_ROWS = N_SLOTS // LANES
ROW_CHUNKS = D // LANES

SC_CORES = 2
SC_SUBCORES = 16
SC_WORKERS = SC_CORES * SC_SUBCORES
SC_ROWS = 32

VMEM_LIMIT = 56 * 1024 * 1024


def _sigmoid(v):
    return 1.0 / (1.0 + jnp.exp(-v))


def _softplus(v):
    return jnp.maximum(v, 0.0) + jnp.log(1.0 + jnp.exp(-jnp.abs(v)))


def _resident(shape):
    nd = len(shape)
    return pl.BlockSpec(shape, lambda *_: (0,) * nd, pipeline_mode=pl.Buffered(1))


def _adaln_kernel(c_ref, w_ref, b_ref, o_ref):
    c = c_ref[...]
    act = c * _sigmoid(c)
    o_ref[...] = jnp.dot(act, w_ref[...], precision=HIGHEST,
                         preferred_element_type=F32) + b_ref[...]


def _adaln(c, w_ada, b_ada):
    tn = 1536
    return pl.pallas_call(
        _adaln_kernel,
        grid=(6 * D // tn,),
        in_specs=[pl.BlockSpec((NB, D), lambda j: (0, 0)),
                  pl.BlockSpec((D, tn), lambda j: (0, j)),
                  pl.BlockSpec((1, tn), lambda j: (0, j))],
        out_specs=pl.BlockSpec((NB, tn), lambda j: (0, j)),
        out_shape=jax.ShapeDtypeStruct((NB, 6 * D), F32),
        compiler_params=pltpu.CompilerParams(vmem_limit_bytes=VMEM_LIMIT),
        name="adaln",
    )(c, w_ada, b_ada)


def _inproj_kernel(x_ref, mod_ref, nw_ref, wz_ref, wx_ref, wdt_ref, wp_ref, wg_ref,
                   z_ref, xbc_ref, dt_ref, up_ref, g_ref):
    x = x_ref[0]
    ms = jnp.mean(x * x, axis=-1, keepdims=True)
    h = x * lax.rsqrt(ms + EPS) * nw_ref[...]
    h = h * (1.0 + mod_ref[0, 1:2, :]) + mod_ref[0, 0:1, :]
    hb = h.astype(BF16)
    z_ref[0] = jnp.dot(hb, wz_ref[...], preferred_element_type=F32).astype(BF16)
    xbc_ref[0] = jnp.dot(hb, wx_ref[...], preferred_element_type=F32).astype(BF16)
    dt_ref[0] = jnp.dot(hb, wdt_ref[...], preferred_element_type=F32)
    up_ref[0] = jnp.dot(hb, wp_ref[...], preferred_element_type=F32).astype(BF16)
    g_ref[0] = jnp.dot(hb, wg_ref[...], preferred_element_type=F32).astype(BF16)


def _inproj(x, mod, norm_w, wz, wx, wdt, wp, wg):
    tm = TM_PROJ
    tok = lambda n: pl.BlockSpec((1, tm, n), lambda b, i: (b, i, 0))
    return pl.pallas_call(
        _inproj_kernel,
        grid=(NB, S // tm),
        in_specs=[tok(D),
                  pl.BlockSpec((1, 6, D), lambda b, i: (b, 0, 0)),
                  _resident((1, D)),
                  _resident((D, D_SSD)), _resident((D, D_XBC)), _resident((D, LANES)),
                  _resident((D, D)), _resident((D, 2 * D))],
        out_specs=[tok(D_SSD), tok(D_XBC), tok(LANES), tok(D), tok(2 * D)],
        out_shape=[jax.ShapeDtypeStruct((NB, S, D_SSD), BF16),
                   jax.ShapeDtypeStruct((NB, S, D_XBC), BF16),
                   jax.ShapeDtypeStruct((NB, S, LANES), F32),
                   jax.ShapeDtypeStruct((NB, S, D), BF16),
                   jax.ShapeDtypeStruct((NB, S, 2 * D), BF16)],
        compiler_params=pltpu.CompilerParams(
            dimension_semantics=("arbitrary", "arbitrary"), vmem_limit_bytes=VMEM_LIMIT),
        name="inproj",
    )(x, mod, norm_w, wz, wx, wdt, wp, wg)


def _ssd_kernel(xbc_ref, z_ref, dtr_ref, cw_ref, cb_ref, dtb_row_ref, alog_row_ref,
                dtb_col_ref, alog_col_ref, dskip_ref, nw_ref, expand_ref,
                y_ref, ubuf, state, ybuf):
    c = pl.program_id(1)

    @pl.when(c == 0)
    def _():
        ubuf[0:SUBLANES, :] = jnp.zeros((SUBLANES, D_XBC), F32)
        state[...] = jnp.zeros_like(state)

    ubuf[SUBLANES:SUBLANES + CHUNK, :] = xbc_ref[0].astype(F32)
    conv = cb_ref[...] + cw_ref[0:1, :] * ubuf[pl.ds(SUBLANES - 3, CHUNK), :]
    for k in range(1, CONV_K):
        conv = conv + cw_ref[k:k + 1, :] * ubuf[pl.ds(SUBLANES - 3 + k, CHUNK), :]
    ubuf[0:SUBLANES, :] = ubuf[CHUNK:CHUNK + SUBLANES, :]
    xc = conv * _sigmoid(conv)

    row = lax.broadcasted_iota(I32, (CHUNK, CHUNK), 0)
    col = lax.broadcasted_iota(I32, (CHUNK, CHUNK), 1)
    causal = row >= col
    tril = causal.astype(F32)
    triu = (row <= col).astype(F32)

    dtr = dtr_ref[0]
    dt = _softplus(dtr + dtb_row_ref[...])
    a_cum = jnp.dot(tril, dt * (-jnp.exp(alog_row_ref[...])), precision=HIGHEST,
                    preferred_element_type=F32)
    dt_t = _softplus(dtr.T + dtb_col_ref[...])
    a_cum_t = jnp.dot(dt_t * (-jnp.exp(alog_col_ref[...])), triu, precision=HIGHEST,
                      preferred_element_type=F32)
    w_state_t = dt_t * jnp.exp(a_cum_t[:, CHUNK - 1:CHUNK] - a_cum_t)
    exp_a = jnp.exp(a_cum)
    chunk_decay = jnp.dot(exp_a[CHUNK - SUBLANES:CHUNK, :], expand_ref[...], precision=HIGHEST,
                          preferred_element_type=F32)[SUBLANES - 1:SUBLANES, :]

    lane = lax.broadcasted_iota(I32, (CHUNK, LANES), 1)
    first_half = lane < HEAD_DIM

    def block_diag(v):
        zero = jnp.zeros_like(v)
        return jnp.concatenate([jnp.where(first_half, v, zero),
                                jnp.where(first_half, zero, v)], axis=0).astype(BF16)

    for g in range(N_GROUPS):
        b_g = xc[:, D_SSD + g * D_STATE:D_SSD + (g + 1) * D_STATE]
        c_g = xc[:, D_SSD + (N_GROUPS + g) * D_STATE:D_SSD + (N_GROUPS + g + 1) * D_STATE]
        cb = lax.dot_general(c_g.astype(BF16), b_g.astype(BF16), (((1,), (1,)), ((), ())),
                             preferred_element_type=F32)
        b_t = b_g.T
        for j in range(GROUP_W // LANES):
            lo = g * GROUP_W + j * LANES
            xp = xc[:, lo:lo + LANES]
            rhs_x = block_diag(xp)
            prev = state[g, :, j * LANES:(j + 1) * LANES]
            rhs_p = block_diag(prev)
            m_parts, o_parts, s_parts = [], [], []
            for hh in range(2):
                h = (lo // HEAD_DIM) + hh
                seg = a_cum[:, h:h + 1] - a_cum_t[h:h + 1, :]
                decay = jnp.exp(jnp.where(causal, seg, -jnp.inf))
                m_parts.append(cb * decay * dt_t[h:h + 1, :])
                o_parts.append(c_g * exp_a[:, h:h + 1])
                s_parts.append(b_t * w_state_t[h:h + 1, :])
            lhs_y = jnp.concatenate(m_parts, axis=1).astype(BF16)
            lhs_o = jnp.concatenate(o_parts, axis=1).astype(BF16)
            lhs_s = jnp.concatenate(s_parts, axis=1).astype(BF16)
            y_pair = (jnp.dot(lhs_y, rhs_x, preferred_element_type=F32)
                      + jnp.dot(lhs_o, rhs_p, preferred_element_type=F32))
            st_pair = jnp.dot(lhs_s, rhs_x, preferred_element_type=F32)
            state[g, :, j * LANES:(j + 1) * LANES] = (
                prev * chunk_decay[:, lo:lo + LANES] + st_pair)
            ybuf[:, lo:lo + LANES] = y_pair + dskip_ref[:, lo:lo + LANES] * xp

    zf = z_ref[0].astype(F32)
    yg = ybuf[...] * (zf * _sigmoid(zf))
    for g in range(N_GROUPS):
        blk = yg[:, g * GROUP_W:(g + 1) * GROUP_W]
        ms = jnp.mean(blk * blk, axis=-1, keepdims=True)
        y_ref[0, :, g * GROUP_W:(g + 1) * GROUP_W] = (
            blk * lax.rsqrt(ms + EPS) * nw_ref[:, g * GROUP_W:(g + 1) * GROUP_W]).astype(BF16)


def _ssd(xbc, z, dtr, conv_w, conv_b, dtb_row, alog_row, dtb_col, alog_col, dskip, norm_w, expand):
    tok = lambda n: pl.BlockSpec((1, CHUNK, n), lambda b, i: (b, i, 0))
    return pl.pallas_call(
        _ssd_kernel,
        grid=(NB, S // CHUNK),
        in_specs=[tok(D_XBC), tok(D_SSD), tok(LANES),
                  _resident((CONV_K, D_XBC)), _resident((1, D_XBC)),
                  _resident((1, LANES)), _resident((1, LANES)),
                  _resident((LANES, 1)), _resident((LANES, 1)),
                  _resident((1, D_SSD)), _resident((1, D_SSD)),
                  _resident((LANES, D_SSD))],
        out_specs=tok(D_SSD),
        out_shape=jax.ShapeDtypeStruct((NB, S, D_SSD), BF16),
        scratch_shapes=[pltpu.VMEM((SUBLANES + CHUNK, D_XBC), F32),
                        pltpu.VMEM((N_GROUPS, D_STATE, GROUP_W), F32),
                        pltpu.VMEM((CHUNK, D_SSD), F32)],
        compiler_params=pltpu.CompilerParams(
            dimension_semantics=("arbitrary", "arbitrary"), vmem_limit_bytes=VMEM_LIMIT),
        name="ssd",
    )(xbc, z, dtr, conv_w, conv_b, dtb_row, alog_row, dtb_col, alog_col, dskip, norm_w, expand)


def _merge_kernel(y_ref, up_ref, g_ref, x_ref, mod_ref, wso_ref, wpool_ref, pscale_ref,
                  wpo_ref, wout_ref, nw_ref, wr_ref, br_ref,
                  x1_ref, h2_ref, eidx_ref, grow_ref, ubuf):
    i = pl.program_id(1)
    tm = TM_MERGE

    @pl.when(i == 0)
    def _():
        ubuf[0:HALO, :] = jnp.zeros((HALO, D), F32)

    u = up_ref[0].astype(F32)
    ubuf[HALO:HALO + tm, :] = u
    ext = ubuf[...]
    ubuf[0:HALO, :] = ubuf[tm:tm + HALO, :]

    t_pos = i * tm + lax.broadcasted_iota(I32, (tm, 1), 0) + 1
    pooled = []
    for gi, w in enumerate(POOL_WINDOWS):
        acc = ext[:, gi * POOL_GW:(gi + 1) * POOL_GW]
        span = 1
        while span < w:
            acc = acc + pltpu.roll(acc, span, 0)
            span *= 2
        cnt = jnp.minimum(t_pos, w).astype(F32)
        p = acc[HALO:, :] / cnt - u[:, gi * POOL_GW:(gi + 1) * POOL_GW]
        pooled.append(jnp.dot(p.astype(BF16), wpool_ref[gi], preferred_element_type=F32))
    p = jnp.concatenate(pooled, axis=1) * pscale_ref[...]
    y_pool = jnp.dot(p.astype(BF16), wpo_ref[...], preferred_element_type=F32)
    y_ssd = jnp.dot(y_ref[0], wso_ref[...], preferred_element_type=F32)

    gt = _sigmoid(g_ref[0].astype(F32))
    merged = gt[:, :D] * y_ssd + gt[:, D:] * y_pool
    mix = jnp.dot(merged.astype(BF16), wout_ref[...], preferred_element_type=F32)
    x1 = x_ref[0] + mod_ref[0, 2:3, :] * mix
    x1_ref[0] = x1

    ms = jnp.mean(x1 * x1, axis=-1, keepdims=True)
    h2 = x1 * lax.rsqrt(ms + EPS) * nw_ref[...]
    h2 = h2 * (1.0 + mod_ref[0, 4:5, :]) + mod_ref[0, 3:4, :]
    for cc in range(ROW_CHUNKS):
        h2_ref[0, :, cc, :] = h2[:, cc * LANES:(cc + 1) * LANES]

    logits = lax.dot_general(wr_ref[...], h2, (((1,), (1,)), ((), ())), precision=HIGHEST,
                             preferred_element_type=F32) + br_ref[...]
    e_iota = lax.broadcasted_iota(I32, (N_EXP, tm), 0)
    vals, idxs = [], []
    for _ in range(TOP_K):
        m = jnp.max(logits, axis=0, keepdims=True)
        idx = jnp.min(jnp.where(logits == m, e_iota, N_EXP), axis=0, keepdims=True)
        vals.append(m)
        idxs.append(idx)
        logits = jnp.where(e_iota == idx, -jnp.inf, logits)
    exps = [jnp.exp(v - vals[0]) for v in vals]
    denom = exps[0] + exps[1] + exps[2] + exps[3]
    eidx_ref[0] = jnp.concatenate(idxs, axis=0)
    gates_t = jnp.concatenate([e / denom for e in exps]
                              + [jnp.zeros((LANES - TOP_K, tm), F32)], axis=0)
    grow_ref[0] = gates_t.T


def _merge(y, up, gts, x, mod, wso, wpool, pscale, wpo, wout, nw, wr_t, br_col):
    tm = TM_MERGE
    tok = lambda n: pl.BlockSpec((1, tm, n), lambda b, i: (b, i, 0))
    return pl.pallas_call(
        _merge_kernel,
        grid=(NB, S // tm),
        in_specs=[tok(D_SSD), tok(D), tok(2 * D), tok(D),
                  pl.BlockSpec((1, 6, D), lambda b, i: (b, 0, 0)),
                  _resident((D_SSD, D)), _resident((len(POOL_WINDOWS), POOL_GW, POOL_GW)),
                  _resident((1, D)), _resident((D, D)), _resident((D, D)),
                  _resident((1, D)), _resident((N_EXP, D)), _resident((N_EXP, 1))],
        out_specs=[tok(D),
                   pl.BlockSpec((1, tm, ROW_CHUNKS, LANES), lambda b, i: (b, i, 0, 0)),
                   pl.BlockSpec((1, TOP_K, tm), lambda b, i: (b, 0, i)),
                   tok(LANES)],
        out_shape=[jax.ShapeDtypeStruct((NB, S, D), F32),
                   jax.ShapeDtypeStruct((NB, S, ROW_CHUNKS, LANES), F32),
                   jax.ShapeDtypeStruct((NB, TOP_K, S), I32),
                   jax.ShapeDtypeStruct((NB, S, LANES), F32)],
        scratch_shapes=[pltpu.VMEM((HALO + tm, D), F32)],
        compiler_params=pltpu.CompilerParams(
            dimension_semantics=("arbitrary", "arbitrary"), vmem_limit_bytes=VMEM_LIMIT),
        name="merge",
    )(y, up, gts, x, mod, wso, wpool, pscale, wpo, wout, nw, wr_t, br_col)


def _sort_kernel(e_ref, tok_ref, dest_ref, counts_ref):
    row = lax.broadcasted_iota(I32, (SORT_ROWS, LANES), 0)
    lane = lax.broadcasted_iota(I32, (SORT_ROWS, LANES), 1)
    flat = row * LANES + lane
    s_bits = S.bit_length() - 1
    slot = (((flat >> s_bits) & (TOP_K - 1)) * T
            + (flat >> (s_bits + 2)) * S + (flat & (S - 1)))
    e = e_ref[...]
    keys = (e << 16) | slot

    lane_e = lax.broadcasted_iota(I32, (SUBLANES, LANES), 1)
    counts = jnp.zeros((SUBLANES, LANES), I32)
    for ex in range(N_EXP):
        n = jnp.sum((e == ex).astype(F32))
        counts = jnp.where(lane_e == ex, n.astype(I32), counts)
    counts_ref[...] = counts

    def partner(v, d):
        if d < LANES:
            down = pltpu.roll(v, d, 1)
            up = pltpu.roll(v, LANES - d, 1)
            return jnp.where((lane & d) != 0, down, up)
        r = d // LANES
        down = pltpu.roll(v, r, 0)
        up = pltpu.roll(v, SORT_ROWS - r, 0)
        return jnp.where((row & r) != 0, down, up)

    def bitonic_sort(keys):
        k = 2
        while k <= N_SLOTS:
            d = k // 2
            while d >= 1:
                other = partner(keys, d)
                is_low = (flat & d) == 0
                ascending = (flat & k) == 0
                take_min = is_low == ascending
                keys = jnp.where(take_min, jnp.minimum(keys, other), jnp.maximum(keys, other))
                d //= 2
            k *= 2
        return keys

    by_expert = bitonic_sort(keys) & (N_SLOTS - 1)
    tok_ref[...] = by_expert & (T - 1)
    by_slot = bitonic_sort(((by_expert - N_SLOTS // 2) << 16) | flat)
    dest_ref[...] = by_slot & (N_SLOTS - 1)


def _sort(e2d):
    return pl.pallas_call(
        _sort_kernel,
        out_shape=[jax.ShapeDtypeStruct((SORT_ROWS, LANES), I32),
                   jax.ShapeDtypeStruct((SORT_ROWS, LANES), I32),
                   jax.ShapeDtypeStruct((SUBLANES, LANES), I32)],
        compiler_params=pltpu.CompilerParams(vmem_limit_bytes=VMEM_LIMIT),
        name="route_sort",
    )(e2d)


def _sc_gather_rows(table, idx):
    n = idx.shape[0]
    per_worker = n // SC_WORKERS
    n_chunks = per_worker // SC_ROWS
    assert per_worker * SC_WORKERS == n and n_chunks * SC_ROWS == per_worker and n_chunks % 2 == 0
    mesh = plsc.VectorSubcoreMesh(core_axis_name="core", subcore_axis_name="subcore",
                                  num_cores=SC_CORES, num_subcores=SC_SUBCORES)
    row = (SC_ROWS, ROW_CHUNKS, LANES)

    def body(table_hbm, idx_hbm, out_hbm, iv0, iv1, r0, r1, is0, is1, gs0, gs1):
        idx_v, rows, idx_sem, row_sem = (iv0, iv1), (r0, r1), (is0, is1), (gs0, gs1)
        worker = lax.axis_index("subcore") * SC_CORES + lax.axis_index("core")
        base = worker * per_worker

        def idx_copy(i, b):
            return pltpu.make_async_copy(idx_hbm.at[pl.ds(base + i * SC_ROWS, SC_ROWS)],
                                         idx_v[b], idx_sem[b])

        def row_gather(b):
            return pltpu.make_async_copy(table_hbm.at[idx_v[b]], rows[b], row_sem[b])

        idx_copy(0, 0).start()
        idx_copy(0, 0).wait()
        row_gather(0).start()
        idx_copy(1, 1).start()

        @pl.loop(0, n_chunks // 2)
        def _(pair):
            for b in (0, 1):
                i = 2 * pair + b
                row_gather(b).wait()

                @pl.when(i + 1 < n_chunks)
                def _():
                    idx_copy(i + 1, 1 - b).wait()
                    row_gather(1 - b).start()

                @pl.when(i + 2 < n_chunks)
                def _():
                    idx_copy(i + 2, b).start()

                pltpu.sync_copy(rows[b], out_hbm.at[pl.ds(base + i * SC_ROWS, SC_ROWS)])

    return pl.kernel(
        body, mesh=mesh,
        out_type=jax.ShapeDtypeStruct((n, ROW_CHUNKS, LANES), F32),
        scratch_types=[pltpu.VMEM((SC_ROWS,), I32), pltpu.VMEM((SC_ROWS,), I32),
                       pltpu.VMEM(row, F32), pltpu.VMEM(row, F32),
                       pltpu.SemaphoreType.DMA, pltpu.SemaphoreType.DMA,
                       pltpu.SemaphoreType.DMA, pltpu.SemaphoreType.DMA],
    )(table, idx)


def _expert_kernel(bexp_ref, bp0_ref, bend_ref, bfirst_ref, nused_ref,
                   xs_hbm, wgu_ref, bgu_ref, wd_ref, bd_ref, yb_hbm,
                   xbuf, obuf, wgu_bf, wd_bf, in_sem, out_sem):
    j = pl.program_id(0)
    n_used = nused_ref[0]

    def n_rows(jj):
        return jnp.clip(bend_ref[jj] - bp0_ref[jj], 0, BM)

    def slab_in(jj, buf):
        n = n_rows(jj)
        return pltpu.make_async_copy(xs_hbm.at[pl.ds(bp0_ref[jj], n)],
                                     xbuf.at[buf, pl.ds(0, n)], in_sem.at[buf])

    def slab_out(jj, buf):
        n = n_rows(jj)
        return pltpu.make_async_copy(obuf.at[buf, pl.ds(0, n)],
                                     yb_hbm.at[pl.ds(bp0_ref[jj], n)], out_sem.at[buf])

    @pl.when(j == 0)
    def _():
        xbuf[...] = jnp.zeros_like(xbuf)
        slab_in(0, 0).start()

    @pl.when(j < n_used)
    def _():
        buf = j % 2
        slab_in(j, buf).wait()

        @pl.when(j + 1 < n_used)
        def _():
            slab_in(j + 1, 1 - buf).start()

        @pl.when(bfirst_ref[j] == 1)
        def _():
            wgu_bf[...] = wgu_ref[0].astype(BF16)
            wd_bf[...] = wd_ref[0].astype(BF16)

        x = jnp.concatenate([xbuf[buf, :, cc, :] for cc in range(ROW_CHUNKS)],
                            axis=1).astype(BF16)
        gu = jnp.dot(x, wgu_bf[...], preferred_element_type=F32) + bgu_ref[0]
        glu = jnp.minimum(gu[:, :D], SWIGLU_LIMIT)
        lin = jnp.clip(gu[:, D:], -SWIGLU_LIMIT, SWIGLU_LIMIT)
        act = glu * _sigmoid(SWIGLU_ALPHA * glu) * (lin + 1.0)
        y = jnp.dot(act.astype(BF16), wd_bf[...], preferred_element_type=F32) + bd_ref[0]

        @pl.when(j >= 2)
        def _():
            slab_out(j - 2, buf).wait()

        for cc in range(ROW_CHUNKS):
            obuf[buf, :, cc, :] = y[:, cc * LANES:(cc + 1) * LANES]
        slab_out(j, buf).start()

        @pl.when(j == n_used - 1)
        def _():
            slab_out(j, buf).wait()

            @pl.when(j >= 1)
            def _():
                slab_out(j - 1, 1 - buf).wait()


def _experts(bexp, bp0, bend, bfirst, nused, xs, w_gu, b_gu, w_down, b_down):
    grid_spec = pltpu.PrefetchScalarGridSpec(
        num_scalar_prefetch=5,
        grid=(N_BLOCKS,),
        in_specs=[pl.BlockSpec(memory_space=pl.ANY),
                  pl.BlockSpec((1, D, 2 * D), lambda j, be, *_: (be[j], 0, 0)),
                  pl.BlockSpec((1, 1, 2 * D), lambda j, be, *_: (be[j], 0, 0)),
                  pl.BlockSpec((1, D, D), lambda j, be, *_: (be[j], 0, 0)),
                  pl.BlockSpec((1, 1, D), lambda j, be, *_: (be[j], 0, 0))],
        out_specs=pl.BlockSpec(memory_space=pl.ANY),
        scratch_shapes=[pltpu.VMEM((2, BM, ROW_CHUNKS, LANES), F32),
                        pltpu.VMEM((2, BM, ROW_CHUNKS, LANES), F32),
                        pltpu.VMEM((D, 2 * D), BF16),
                        pltpu.VMEM((D, D), BF16),
                        pltpu.SemaphoreType.DMA((2,)),
                        pltpu.SemaphoreType.DMA((2,))],
    )
    return pl.pallas_call(
        _expert_kernel,
        grid_spec=grid_spec,
        out_shape=jax.ShapeDtypeStruct((N_SLOTS, ROW_CHUNKS, LANES), F32),
        compiler_params=pltpu.CompilerParams(
            dimension_semantics=("arbitrary",), vmem_limit_bytes=VMEM_LIMIT),
        name="experts",
    )(bexp, bp0, bend, bfirst, nused, xs, w_gu, b_gu, w_down, b_down)


def _final_kernel(y0_ref, y1_ref, y2_ref, y3_ref, grow_ref, x1_ref, mod_ref, nw_ref, o_ref):
    gr = grow_ref[0]
    gates = [gr[:, k:k + 1] for k in range(TOP_K)]
    y = jnp.concatenate(
        [sum(gates[k] * ref[:, cc, :] for k, ref in enumerate((y0_ref, y1_ref, y2_ref, y3_ref)))
         for cc in range(ROW_CHUNKS)], axis=1)
    x2 = x1_ref[0] + mod_ref[0, 5:6, :] * y
    ms = jnp.mean(x2 * x2, axis=-1, keepdims=True)
    o_ref[0] = x2 * lax.rsqrt(ms + EPS) * nw_ref[...]


def _final(yg, grow, x1, mod, nw):
    tm = TM_FINAL
    per_b = S // tm
    per_k = T // tm

    def yspec(k):
        return pl.BlockSpec((tm, ROW_CHUNKS, LANES),
                            lambda b, i: (k * per_k + b * per_b + i, 0, 0))

    tok = lambda n: pl.BlockSpec((1, tm, n), lambda b, i: (b, i, 0))
    return pl.pallas_call(
        _final_kernel,
        grid=(NB, per_b),
        in_specs=[yspec(0), yspec(1), yspec(2), yspec(3), tok(LANES), tok(D),
                  pl.BlockSpec((1, 6, D), lambda b, i: (b, 0, 0)),
                  _resident((1, D))],
        out_specs=tok(D),
        out_shape=jax.ShapeDtypeStruct((NB, S, D), F32),
        compiler_params=pltpu.CompilerParams(
            dimension_semantics=("arbitrary", "arbitrary"), vmem_limit_bytes=VMEM_LIMIT),
        name="final",
    )(yg, yg, yg, yg, grow, x1, mod, nw)


def _block_tables(counts):
    nblk = (counts + BM - 1) // BM
    blk_cum = jnp.cumsum(nblk)
    n_used = blk_cum[-1]
    blk_start = blk_cum - nblk
    start = jnp.cumsum(counts) - counts
    jj = jnp.arange(N_BLOCKS, dtype=I32)
    jc = jnp.minimum(jj, n_used - 1)[:, None]
    owner = ((blk_start[None, :] <= jc) & (jc < blk_cum[None, :])).astype(I32)
    pick = lambda v: jnp.sum(owner * v[None, :], axis=1)
    e_j = pick(jnp.arange(N_EXP, dtype=I32))
    bp0 = pick(start) + (jj - pick(blk_start)) * BM
    bend = pick(start + counts)
    bfirst = (jj == pick(blk_start)).astype(I32)
    return e_j, bp0.astype(I32), bend.astype(I32), bfirst, n_used.reshape(1).astype(I32)


def kernel(x, c, w_ada, b_ada, norm_mix_w, w_in, conv_w, conv_b, dt_bias, a_log, d_skip,
           ssd_norm_w, w_ssd_out, w_pool, pool_scale, w_pool_out, w_out, norm_ffn_w,
           w_router, b_router, w_gu, b_gu, w_down, b_down, norm_final_w):
    assert x.shape == (NB, S, D) and w_ada.shape[0] == 1

    mod = _adaln(c, w_ada[0], b_ada).reshape(NB, 6, D)

    w_in_b = w_in[0].astype(BF16)
    o1, o2, o3, o4 = D_SSD, D_SSD + D_XBC, D_SSD + D_XBC + N_HEADS, D_SSD + D_XBC + N_HEADS + D
    wz, wx, wp, wg = w_in_b[:, :o1], w_in_b[:, o1:o2], w_in_b[:, o3:o4], w_in_b[:, o4:]
    wdt = jnp.pad(w_in_b[:, o2:o3], ((0, 0), (0, LANES - N_HEADS)))
    z, xbc, dtr, up, gts = _inproj(x, mod, norm_mix_w, wz, wx, wdt, wp, wg)

    pad_h = lambda v: jnp.pad(v[0], (0, LANES - N_HEADS))
    dtb, alog = pad_h(dt_bias), pad_h(a_log)
    head_of_chan = jnp.arange(D_SSD, dtype=I32) // HEAD_DIM
    expand = (jnp.arange(LANES, dtype=I32)[:, None] == head_of_chan[None, :]).astype(F32)
    dskip = jnp.repeat(d_skip[0], HEAD_DIM).reshape(1, D_SSD)
    y = _ssd(xbc, z, dtr, conv_w[0], conv_b, dtb.reshape(1, LANES), alog.reshape(1, LANES),
             dtb.reshape(LANES, 1), alog.reshape(LANES, 1), dskip, ssd_norm_w, expand)

    x1, h2, eidx, grow = _merge(
        y, up, gts, x, mod, w_ssd_out[0].astype(BF16), w_pool[0].astype(BF16), pool_scale,
        w_pool_out[0].astype(BF16), w_out[0].astype(BF16), norm_ffn_w,
        w_router[0].T, b_router.reshape(N_EXP, 1))

    tok_of_row, row_of_slot, counts = _sort(eidx.reshape(SORT_ROWS, LANES))
    bexp, bp0, bend, bfirst, nused = _block_tables(counts[0, :N_EXP])

    xs = _sc_gather_rows(h2.reshape(T, ROW_CHUNKS, LANES), tok_of_row.reshape(N_SLOTS))
    yb = _experts(bexp, bp0, bend, bfirst, nused, xs, w_gu[0],
                  b_gu[0].reshape(N_EXP, 1, 2 * D), w_down[0], b_down[0].reshape(N_EXP, 1, D))
    yg = _sc_gather_rows(yb, row_of_slot.reshape(N_SLOTS))

    return _final(yg, grow, x1, mod, norm_final_w.reshape(1, D))
```

```python
import functools

import jax
import jax.numpy as jnp
from jax import lax
from jax.experimental import pallas as pl
from jax.experimental.pallas import tpu as pltpu
from jax.experimental.pallas import tpu_sc as plsc

F32 = jnp.float32
BF16 = jnp.bfloat16
I32 = jnp.int32
HIGHEST = lax.Precision.HIGHEST

D = 1024
NB = 4
S = 4096
T = NB * S
EPS = 1e-5
D_SSD = 2048
N_HEADS = 32
HEAD_DIM = 64
N_GROUPS = 4
D_STATE = 128
GROUP_W = D_SSD // N_GROUPS
D_XBC = D_SSD + 2 * N_GROUPS * D_STATE
CONV_K = 4
POOL_WINDOWS = (2, 4, 8, 16)
POOL_GW = 256
N_EXP = 32
TOP_K = 4
SWIGLU_LIMIT = 7.0
SWIGLU_ALPHA = 1.702

LANES = 128
SUBLANES = 8

CHUNK = 128
TM_PROJ = 512
TM_MERGE = 512
TM_FINAL = 512
BM = 256
N_SLOTS = T * TOP_K
N_BLOCKS = N_SLOTS // BM + N_EXP
HALO = 16
SORT_ROWS = N_SLOTS // LANES
ROW_CHUNKS = D // LANES

SC_CORES = 2
SC_SUBCORES = 16
SC_WORKERS = SC_CORES * SC_SUBCORES
SC_ROWS = 32

VMEM_LIMIT = 56 * 1024 * 1024


def _sigmoid(v):
    return 1.0 / (1.0 + jnp.exp(-v))


def _softplus(v):
    return jnp.maximum(v, 0.0) + jnp.log(1.0 + jnp.exp(-jnp.abs(v)))


def _resident(shape):
    nd = len(shape)
    return pl.BlockSpec(shape, lambda *_: (0,) * nd, pipeline_mode=pl.Buffered(1))


def _adaln_kernel(c_ref, w_ref, b_ref, o_ref):
    c = c_ref[...]
    act = c * _sigmoid(c)
    o_ref[...] = jnp.dot(act, w_ref[...], precision=HIGHEST,
                         preferred_element_type=F32) + b_ref[...]


def _adaln(c, w_ada, b_ada):
    tn = 1536
    return pl.pallas_call(
        _adaln_kernel,
        grid=(6 * D // tn,),
        in_specs=[pl.BlockSpec((NB, D), lambda j: (0, 0)),
                  pl.BlockSpec((D, tn), lambda j: (0, j)),
                  pl.BlockSpec((1, tn), lambda j: (0, j))],
        out_specs=pl.BlockSpec((NB, tn), lambda j: (0, j)),
        out_shape=jax.ShapeDtypeStruct((NB, 6 * D), F32),
        compiler_params=pltpu.CompilerParams(vmem_limit_bytes=VMEM_LIMIT),
        name="adaln",
    )(c, w_ada, b_ada)


def _inproj_kernel(x_ref, mod_ref, nw_ref, wz_ref, wx_ref, wdt_ref, wp_ref, wg_ref,
                   z_ref, xbc_ref, dt_ref, up_ref, g_ref):
    x = x_ref[0]
    ms = jnp.mean(x * x, axis=-1, keepdims=True)
    h = x * lax.rsqrt(ms + EPS) * nw_ref[...]
    h = h * (1.0 + mod_ref[0, 1:2, :]) + mod_ref[0, 0:1, :]
    hb = h.astype(BF16)
    z_ref[0] = jnp.dot(hb, wz_ref[...], preferred_element_type=F32).astype(BF16)
    xbc_ref[0] = jnp.dot(hb, wx_ref[...], preferred_element_type=F32).astype(BF16)
    dt_ref[0] = jnp.dot(hb, wdt_ref[...], preferred_element_type=F32)
    up_ref[0] = jnp.dot(hb, wp_ref[...], preferred_element_type=F32).astype(BF16)
    g_ref[0] = jnp.dot(hb, wg_ref[...], preferred_element_type=F32).astype(BF16)


def _inproj(x, mod, norm_w, wz, wx, wdt, wp, wg):
    tm = TM_PROJ
    tok = lambda n: pl.BlockSpec((1, tm, n), lambda b, i: (b, i, 0))
    return pl.pallas_call(
        _inproj_kernel,
        grid=(NB, S // tm),
        in_specs=[tok(D),
                  pl.BlockSpec((1, 6, D), lambda b, i: (b, 0, 0)),
                  _resident((1, D)),
                  _resident((D, D_SSD)), _resident((D, D_XBC)), _resident((D, LANES)),
                  _resident((D, D)), _resident((D, 2 * D))],
        out_specs=[tok(D_SSD), tok(D_XBC), tok(LANES), tok(D), tok(2 * D)],
        out_shape=[jax.ShapeDtypeStruct((NB, S, D_SSD), BF16),
                   jax.ShapeDtypeStruct((NB, S, D_XBC), BF16),
                   jax.ShapeDtypeStruct((NB, S, LANES), F32),
                   jax.ShapeDtypeStruct((NB, S, D), BF16),
                   jax.ShapeDtypeStruct((NB, S, 2 * D), BF16)],
        compiler_params=pltpu.CompilerParams(
            dimension_semantics=("arbitrary", "arbitrary"), vmem_limit_bytes=VMEM_LIMIT),
        name="inproj",
    )(x, mod, norm_w, wz, wx, wdt, wp, wg)


def _ssd_kernel(xbc_ref, z_ref, dtr_ref, cw_ref, cb_ref, dtb_row_ref, alog_row_ref,
                dtb_col_ref, alog_col_ref, dskip_ref, nw_ref, expand_ref,
                y_ref, ubuf, state, ybuf):
    c = pl.program_id(1)

    @pl.when(c == 0)
    def _():
        ubuf[0:SUBLANES, :] = jnp.zeros((SUBLANES, D_XBC), F32)
        state[...] = jnp.zeros_like(state)

    ubuf[SUBLANES:SUBLANES + CHUNK, :] = xbc_ref[0].astype(F32)
    conv = cb_ref[...] + cw_ref[0:1, :] * ubuf[pl.ds(SUBLANES - 3, CHUNK), :]
    for k in range(1, CONV_K):
        conv = conv + cw_ref[k:k + 1, :] * ubuf[pl.ds(SUBLANES - 3 + k, CHUNK), :]
    ubuf[0:SUBLANES, :] = ubuf[CHUNK:CHUNK + SUBLANES, :]
    xc = conv * _sigmoid(conv)

    row = lax.broadcasted_iota(I32, (CHUNK, CHUNK), 0)
    col = lax.broadcasted_iota(I32, (CHUNK, CHUNK), 1)
    causal = row >= col
    tril = causal.astype(F32)
    triu = (row <= col).astype(F32)

    dtr = dtr_ref[0]
    dt = _softplus(dtr + dtb_row_ref[...])
    a_cum = jnp.dot(tril, dt * (-jnp.exp(alog_row_ref[...])), precision=HIGHEST,
                    preferred_element_type=F32)
    dt_t = _softplus(dtr.T + dtb_col_ref[...])
    a_cum_t = jnp.dot(dt_t * (-jnp.exp(alog_col_ref[...])), triu, precision=HIGHEST,
                      preferred_element_type=F32)
    w_state_t = dt_t * jnp.exp(a_cum_t[:, CHUNK - 1:CHUNK] - a_cum_t)
    exp_a = jnp.exp(a_cum)
    chunk_decay = jnp.dot(exp_a[CHUNK - SUBLANES:CHUNK, :], expand_ref[...], precision=HIGHEST,
                          preferred_element_type=F32)[SUBLANES - 1:SUBLANES, :]

    lane = lax.broadcasted_iota(I32, (CHUNK, LANES), 1)
    first_half = lane < HEAD_DIM

    def block_diag(v):
        zero = jnp.zeros_like(v)
        return jnp.concatenate([jnp.where(first_half, v, zero),
                                jnp.where(first_half, zero, v)], axis=0).astype(BF16)

    for g in range(N_GROUPS):
        b_g = xc[:, D_SSD + g * D_STATE:D_SSD + (g + 1) * D_STATE]
        c_g = xc[:, D_SSD + (N_GROUPS + g) * D_STATE:D_SSD + (N_GROUPS + g + 1) * D_STATE]
        cb = lax.dot_general(c_g.astype(BF16), b_g.astype(BF16), (((1,), (1,)), ((), ())),
                             preferred_element_type=F32)
        b_t = b_g.T
        for j in range(GROUP_W // LANES):
            lo = g * GROUP_W + j * LANES
            xp = xc[:, lo:lo + LANES]
            rhs_x = block_diag(xp)
            prev = state[g, :, j * LANES:(j + 1) * LANES]
            rhs_p = block_diag(prev)
            m_parts, o_parts, s_parts = [], [], []
            for hh in range(2):
                h = (lo // HEAD_DIM) + hh
                seg = a_cum[:, h:h + 1] - a_cum_t[h:h + 1, :]
                decay = jnp.exp(jnp.where(causal, seg, -jnp.inf))
                m_parts.append(cb * decay * dt_t[h:h + 1, :])
                o_parts.append(c_g * exp_a[:, h:h + 1])
                s_parts.append(b_t * w_state_t[h:h + 1, :])
            lhs_y = jnp.concatenate(m_parts, axis=1).astype(BF16)
            lhs_o = jnp.concatenate(o_parts, axis=1).astype(BF16)
            lhs_s = jnp.concatenate(s_parts, axis=1).astype(BF16)
            y_pair = (jnp.dot(lhs_y, rhs_x, preferred_element_type=F32)
                      + jnp.dot(lhs_o, rhs_p, preferred_element_type=F32))
            st_pair = jnp.dot(lhs_s, rhs_x, preferred_element_type=F32)
            state[g, :, j * LANES:(j + 1) * LANES] = (
                prev * chunk_decay[:, lo:lo + LANES] + st_pair)
            ybuf[:, lo:lo + LANES] = y_pair + dskip_ref[:, lo:lo + LANES] * xp

    zf = z_ref[0].astype(F32)
    yg = ybuf[...] * (zf * _sigmoid(zf))
    for g in range(N_GROUPS):
        blk = yg[:, g * GROUP_W:(g + 1) * GROUP_W]
        ms = jnp.mean(blk * blk, axis=-1, keepdims=True)
        y_ref[0, :, g * GROUP_W:(g + 1) * GROUP_W] = (
            blk * lax.rsqrt(ms + EPS) * nw_ref[:, g * GROUP_W:(g + 1) * GROUP_W]).astype(BF16)


def _ssd(xbc, z, dtr, conv_w, conv_b, dtb_row, alog_row, dtb_col, alog_col, dskip, norm_w, expand):
    tok = lambda n: pl.BlockSpec((1, CHUNK, n), lambda b, i: (b, i, 0))
    return pl.pallas_call(
        _ssd_kernel,
        grid=(NB, S // CHUNK),
        in_specs=[tok(D_XBC), tok(D_SSD), tok(LANES),
                  _resident((CONV_K, D_XBC)), _resident((1, D_XBC)),
                  _resident((1, LANES)), _resident((1, LANES)),
                  _resident((LANES, 1)), _resident((LANES, 1)),
                  _resident((1, D_SSD)), _resident((1, D_SSD)),
                  _resident((LANES, D_SSD))],
        out_specs=tok(D_SSD),
        out_shape=jax.ShapeDtypeStruct((NB, S, D_SSD), BF16),
        scratch_shapes=[pltpu.VMEM((SUBLANES + CHUNK, D_XBC), F32),
                        pltpu.VMEM((N_GROUPS, D_STATE, GROUP_W), F32),
                        pltpu.VMEM((CHUNK, D_SSD), F32)],
        compiler_params=pltpu.CompilerParams(
            dimension_semantics=("arbitrary", "arbitrary"), vmem_limit_bytes=VMEM_LIMIT),
        name="ssd",
    )(xbc, z, dtr, conv_w, conv_b, dtb_row, alog_row, dtb_col, alog_col, dskip, norm_w, expand)


def _merge_kernel(y_ref, up_ref, g_ref, x_ref, mod_ref, wso_ref, wpool_ref, pscale_ref,
                  wpo_ref, wout_ref, nw_ref, wr_ref, br_ref,
                  x1_ref, h2_ref, eidx_ref, grow_ref, ubuf):
    i = pl.program_id(1)
    tm = TM_MERGE

    @pl.when(i == 0)
    def _():
        ubuf[0:HALO, :] = jnp.zeros((HALO, D), F32)

    u = up_ref[0].astype(F32)
    ubuf[HALO:HALO + tm, :] = u
    ext = ubuf[...]
    ubuf[0:HALO, :] = ubuf[tm:tm + HALO, :]

    t_pos = i * tm + lax.broadcasted_iota(I32, (tm, 1), 0) + 1
    pooled = []
    for gi, w in enumerate(POOL_WINDOWS):
        acc = ext[:, gi * POOL_GW:(gi + 1) * POOL_GW]
        span = 1
        while span < w:
            acc = acc + pltpu.roll(acc, span, 0)
            span *= 2
        cnt = jnp.minimum(t_pos, w).astype(F32)
        p = acc[HALO:, :] / cnt - u[:, gi * POOL_GW:(gi + 1) * POOL_GW]
        pooled.append(jnp.dot(p.astype(BF16), wpool_ref[gi], preferred_element_type=F32))
    p = jnp.concatenate(pooled, axis=1) * pscale_ref[...]
    y_pool = jnp.dot(p.astype(BF16), wpo_ref[...], preferred_element_type=F32)
    y_ssd = jnp.dot(y_ref[0], wso_ref[...], preferred_element_type=F32)

    gt = _sigmoid(g_ref[0].astype(F32))
    merged = gt[:, :D] * y_ssd + gt[:, D:] * y_pool
    mix = jnp.dot(merged.astype(BF16), wout_ref[...], preferred_element_type=F32)
    x1 = x_ref[0] + mod_ref[0, 2:3, :] * mix
    x1_ref[0] = x1

    ms = jnp.mean(x1 * x1, axis=-1, keepdims=True)
    h2 = x1 * lax.rsqrt(ms + EPS) * nw_ref[...]
    h2 = h2 * (1.0 + mod_ref[0, 4:5, :]) + mod_ref[0, 3:4, :]
    for cc in range(ROW_CHUNKS):
        h2_ref[0, pl.ds(cc, tm, stride=ROW_CHUNKS), :] = h2[:, cc * LANES:(cc + 1) * LANES]

    logits = lax.dot_general(wr_ref[...], h2, (((1,), (1,)), ((), ())), precision=HIGHEST,
                             preferred_element_type=F32) + br_ref[...]
    e_iota = lax.broadcasted_iota(I32, (N_EXP, tm), 0)
    vals, idxs = [], []
    for _ in range(TOP_K):
        m = jnp.max(logits, axis=0, keepdims=True)
        idx = jnp.min(jnp.where(logits == m, e_iota, N_EXP), axis=0, keepdims=True)
        vals.append(m)
        idxs.append(idx)
        logits = jnp.where(e_iota == idx, -jnp.inf, logits)
    exps = [jnp.exp(v - vals[0]) for v in vals]
    denom = exps[0] + exps[1] + exps[2] + exps[3]
    eidx_ref[0] = jnp.concatenate(idxs, axis=0)
    gates_t = jnp.concatenate([e / denom for e in exps]
                              + [jnp.zeros((LANES - TOP_K, tm), F32)], axis=0)
    grow_ref[0] = gates_t.T


def _merge(y, up, gts, x, mod, wso, wpool, pscale, wpo, wout, nw, wr_t, br_col):
    tm = TM_MERGE
    tok = lambda n: pl.BlockSpec((1, tm, n), lambda b, i: (b, i, 0))
    return pl.pallas_call(
        _merge_kernel,
        grid=(NB, S // tm),
        in_specs=[tok(D_SSD), tok(D), tok(2 * D), tok(D),
                  pl.BlockSpec((1, 6, D), lambda b, i: (b, 0, 0)),
                  _resident((D_SSD, D)), _resident((len(POOL_WINDOWS), POOL_GW, POOL_GW)),
                  _resident((1, D)), _resident((D, D)), _resident((D, D)),
                  _resident((1, D)), _resident((N_EXP, D)), _resident((N_EXP, 1))],
        out_specs=[tok(D),
                   pl.BlockSpec((1, tm * ROW_CHUNKS, LANES), lambda b, i: (b, i, 0)),
                   pl.BlockSpec((1, TOP_K, tm), lambda b, i: (b, 0, i)),
                   tok(LANES)],
        out_shape=[jax.ShapeDtypeStruct((NB, S, D), F32),
                   jax.ShapeDtypeStruct((NB, S * ROW_CHUNKS, LANES), F32),
                   jax.ShapeDtypeStruct((NB, TOP_K, S), I32),
                   jax.ShapeDtypeStruct((NB, S, LANES), F32)],
        scratch_shapes=[pltpu.VMEM((HALO + tm, D), F32)],
        compiler_params=pltpu.CompilerParams(
            dimension_semantics=("arbitrary", "arbitrary"), vmem_limit_bytes=VMEM_LIMIT),
        name="merge",
    )(y, up, gts, x, mod, wso, wpool, pscale, wpo, wout, nw, wr_t, br_col)


def _sort_kernel(e_ref, tok_ref, dest_ref, counts_ref):
    row = lax.broadcasted_iota(I32, (SORT_ROWS, LANES), 0)
    lane = lax.broadcasted_iota(I32, (SORT_ROWS, LANES), 1)
    flat = row * LANES + lane
    s_bits = S.bit_length() - 1
    slot = (((flat >> s_bits) & (TOP_K - 1)) * T
            + (flat >> (s_bits + 2)) * S + (flat & (S - 1)))
    e = e_ref[...]
    keys = (e << 16) | slot

    lane_e = lax.broadcasted_iota(I32, (SUBLANES, LANES), 1)
    counts = jnp.zeros((SUBLANES, LANES), I32)
    for ex in range(N_EXP):
        n = jnp.sum((e == ex).astype(F32))
        counts = jnp.where(lane_e == ex, n.astype(I32), counts)
    counts_ref[...] = counts

    def partner(v, d):
        if d < LANES:
            down = pltpu.roll(v, d, 1)
            up = pltpu.roll(v, LANES - d, 1)
            return jnp.where((lane & d) != 0, down, up)
        r = d // LANES
        down = pltpu.roll(v, r, 0)
        up = pltpu.roll(v, SORT_ROWS - r, 0)
        return jnp.where((row & r) != 0, down, up)

    def bitonic_sort(keys):
        k = 2
        while k <= N_SLOTS:
            d = k // 2
            while d >= 1:
                other = partner(keys, d)
                is_low = (flat & d) == 0
                ascending = (flat & k) == 0
                take_min = is_low == ascending
                keys = jnp.where(take_min, jnp.minimum(keys, other), jnp.maximum(keys, other))
                d //= 2
            k *= 2
        return keys

    by_expert = bitonic_sort(keys) & (N_SLOTS - 1)
    tok_ref[...] = by_expert & (T - 1)
    by_slot = bitonic_sort(((by_expert - N_SLOTS // 2) << 16) | flat)
    dest_ref[...] = by_slot & (N_SLOTS - 1)


def _sort(e2d):
    return pl.pallas_call(
        _sort_kernel,
        out_shape=[jax.ShapeDtypeStruct((SORT_ROWS, LANES), I32),
                   jax.ShapeDtypeStruct((SORT_ROWS, LANES), I32),
                   jax.ShapeDtypeStruct((SUBLANES, LANES), I32)],
        compiler_params=pltpu.CompilerParams(vmem_limit_bytes=VMEM_LIMIT),
        name="route_sort",
    )(e2d)


def _sc_gather_rows(table, idx):
    n = idx.shape[0]
    per_worker = n // SC_WORKERS
    n_chunks = per_worker // SC_ROWS
    assert per_worker * SC_WORKERS == n and n_chunks * SC_ROWS == per_worker and n_chunks % 2 == 0
    mesh = plsc.VectorSubcoreMesh(core_axis_name="core", subcore_axis_name="subcore",
                                  num_cores=SC_CORES, num_subcores=SC_SUBCORES)
    row = (SC_ROWS, ROW_CHUNKS, LANES)

    def body(table_hbm, idx_hbm, out_hbm, iv0, iv1, r0, r1, is0, is1, gs0, gs1):
        idx_v, rows, idx_sem, row_sem = (iv0, iv1), (r0, r1), (is0, is1), (gs0, gs1)
        worker = lax.axis_index("subcore") * SC_CORES + lax.axis_index("core")
        base = worker * per_worker

        def idx_copy(i, b):
            return pltpu.make_async_copy(idx_hbm.at[pl.ds(base + i * SC_ROWS, SC_ROWS)],
                                         idx_v[b], idx_sem[b])

        def row_gather(b):
            return pltpu.make_async_copy(table_hbm.at[idx_v[b]], rows[b], row_sem[b])

        idx_copy(0, 0).start()
        idx_copy(0, 0).wait()
        row_gather(0).start()
        idx_copy(1, 1).start()

        @pl.loop(0, n_chunks // 2)
        def _(pair):
            for b in (0, 1):
                i = 2 * pair + b
                row_gather(b).wait()

                @pl.when(i + 1 < n_chunks)
                def _():
                    idx_copy(i + 1, 1 - b).wait()
                    row_gather(1 - b).start()

                @pl.when(i + 2 < n_chunks)
                def _():
                    idx_copy(i + 2, b).start()

                pltpu.sync_copy(rows[b], out_hbm.at[pl.ds(base + i * SC_ROWS, SC_ROWS)])

    return pl.kernel(
        body, mesh=mesh,
        out_type=jax.ShapeDtypeStruct((n, ROW_CHUNKS, LANES), F32),
        scratch_types=[pltpu.VMEM((SC_ROWS,), I32), pltpu.VMEM((SC_ROWS,), I32),
                       pltpu.VMEM(row, F32), pltpu.VMEM(row, F32),
                       pltpu.SemaphoreType.DMA, pltpu.SemaphoreType.DMA,
                       pltpu.SemaphoreType.DMA, pltpu.SemaphoreType.DMA],
    )(table, idx)


def _expert_kernel(bexp_ref, bp0_ref, bend_ref, bfirst_ref, nused_ref,
                   xs_hbm, wgu_ref, bgu_ref, wd_ref, bd_ref, yb_hbm,
                   xbuf, obuf, wgu_bf, wd_bf, in_sem, out_sem):
    j = pl.program_id(0)
    n_used = nused_ref[0]

    def n_rows(jj):
        return jnp.clip(bend_ref[jj] - bp0_ref[jj], 0, BM)

    def slab_in(jj, buf):
        n = pl.multiple_of(n_rows(jj) * ROW_CHUNKS, ROW_CHUNKS)
        p0 = pl.multiple_of(bp0_ref[jj] * ROW_CHUNKS, ROW_CHUNKS)
        return pltpu.make_async_copy(xs_hbm.at[pl.ds(p0, n)],
                                     xbuf.at[buf, pl.ds(0, n)], in_sem.at[buf])

    def slab_out(jj, buf):
        n = pl.multiple_of(n_rows(jj) * ROW_CHUNKS, ROW_CHUNKS)
        p0 = pl.multiple_of(bp0_ref[jj] * ROW_CHUNKS, ROW_CHUNKS)
        return pltpu.make_async_copy(obuf.at[buf, pl.ds(0, n)],
                                     yb_hbm.at[pl.ds(p0, n)], out_sem.at[buf])

    @pl.when(j == 0)
    def _():
        xbuf[...] = jnp.zeros_like(xbuf)
        slab_in(0, 0).start()

    @pl.when(j < n_used)
    def _():
        buf = j % 2
        slab_in(j, buf).wait()

        @pl.when(j + 1 < n_used)
        def _():
            slab_in(j + 1, 1 - buf).start()

        @pl.when(bfirst_ref[j] == 1)
        def _():
            wgu_bf[...] = wgu_ref[0].astype(BF16)
            wd_bf[...] = wd_ref[0].astype(BF16)

        x = jnp.concatenate([xbuf[buf, pl.ds(cc, BM, stride=ROW_CHUNKS), :]
                             for cc in range(ROW_CHUNKS)], axis=1).astype(BF16)
        gu = jnp.dot(x, wgu_bf[...], preferred_element_type=F32) + bgu_ref[0]
        glu = jnp.minimum(gu[:, :D], SWIGLU_LIMIT)
        lin = jnp.clip(gu[:, D:], -SWIGLU_LIMIT, SWIGLU_LIMIT)
        act = glu * _sigmoid(SWIGLU_ALPHA * glu) * (lin + 1.0)
        y = jnp.dot(act.astype(BF16), wd_bf[...], preferred_element_type=F32) + bd_ref[0]

        @pl.when(j >= 2)
        def _():
            slab_out(j - 2, buf).wait()

        for cc in range(ROW_CHUNKS):
            obuf[buf, pl.ds(cc, BM, stride=ROW_CHUNKS), :] = y[:, cc * LANES:(cc + 1) * LANES]
        slab_out(j, buf).start()

        @pl.when(j == n_used - 1)
        def _():
            slab_out(j, buf).wait()

            @pl.when(j >= 1)
            def _():
                slab_out(j - 1, 1 - buf).wait()


def _experts(bexp, bp0, bend, bfirst, nused, xs, w_gu, b_gu, w_down, b_down):
    grid_spec = pltpu.PrefetchScalarGridSpec(
        num_scalar_prefetch=5,
        grid=(N_BLOCKS,),
        in_specs=[pl.BlockSpec(memory_space=pl.ANY),
                  pl.BlockSpec((1, D, 2 * D), lambda j, be, *_: (be[j], 0, 0)),
                  pl.BlockSpec((1, 1, 2 * D), lambda j, be, *_: (be[j], 0, 0)),
                  pl.BlockSpec((1, D, D), lambda j, be, *_: (be[j], 0, 0)),
                  pl.BlockSpec((1, 1, D), lambda j, be, *_: (be[j], 0, 0))],
        out_specs=pl.BlockSpec(memory_space=pl.ANY),
        scratch_shapes=[pltpu.VMEM((2, BM * ROW_CHUNKS, LANES), F32),
                        pltpu.VMEM((2, BM * ROW_CHUNKS, LANES), F32),
                        pltpu.VMEM((D, 2 * D), BF16),
                        pltpu.VMEM((D, D), BF16),
                        pltpu.SemaphoreType.DMA((2,)),
                        pltpu.SemaphoreType.DMA((2,))],
    )
    return pl.pallas_call(
        _expert_kernel,
        grid_spec=grid_spec,
        out_shape=jax.ShapeDtypeStruct((N_SLOTS * ROW_CHUNKS, LANES), F32),
        compiler_params=pltpu.CompilerParams(
            dimension_semantics=("arbitrary",), vmem_limit_bytes=VMEM_LIMIT),
        name="experts",
    )(bexp, bp0, bend, bfirst, nused, xs, w_gu, b_gu, w_down, b_down)


def _final_kernel(y0_ref, y1_ref, y2_ref, y3_ref, grow_ref, x1_ref, mod_ref, nw_ref, o_ref):
    gr = grow_ref[0]
    gates = [gr[:, k:k + 1] for k in range(TOP_K)]
    y = jnp.concatenate(
        [sum(gates[k] * ref[pl.ds(cc, TM_FINAL, stride=ROW_CHUNKS), :]
             for k, ref in enumerate((y0_ref, y1_ref, y2_ref, y3_ref)))
         for cc in range(ROW_CHUNKS)], axis=1)
    x2 = x1_ref[0] + mod_ref[0, 5:6, :] * y
    ms = jnp.mean(x2 * x2, axis=-1, keepdims=True)
    o_ref[0] = x2 * lax.rsqrt(ms + EPS) * nw_ref[...]


def _final(yg, grow, x1, mod, nw):
    tm = TM_FINAL
    per_b = S // tm
    per_k = T // tm

    def yspec(k):
        return pl.BlockSpec((tm * ROW_CHUNKS, LANES),
                            lambda b, i: (k * per_k + b * per_b + i, 0))

    tok = lambda n: pl.BlockSpec((1, tm, n), lambda b, i: (b, i, 0))
    return pl.pallas_call(
        _final_kernel,
        grid=(NB, per_b),
        in_specs=[yspec(0), yspec(1), yspec(2), yspec(3), tok(LANES), tok(D),
                  pl.BlockSpec((1, 6, D), lambda b, i: (b, 0, 0)),
                  _resident((1, D))],
        out_specs=tok(D),
        out_shape=jax.ShapeDtypeStruct((NB, S, D), F32),
        compiler_params=pltpu.CompilerParams(
            dimension_semantics=("arbitrary", "arbitrary"), vmem_limit_bytes=VMEM_LIMIT),
        name="final",
    )(yg, yg, yg, yg, grow, x1, mod, nw)


def _block_tables(counts):
    nblk = (counts + BM - 1) // BM
    blk_cum = jnp.cumsum(nblk)
    n_used = blk_cum[-1]
    blk_start = blk_cum - nblk
    start = jnp.cumsum(counts) - counts
    jj = jnp.arange(N_BLOCKS, dtype=I32)
    jc = jnp.minimum(jj, n_used - 1)[:, None]
    owner = ((blk_start[None, :] <= jc) & (jc < blk_cum[None, :])).astype(I32)
    pick = lambda v: jnp.sum(owner * v[None, :], axis=1)
    e_j = pick(jnp.arange(N_EXP, dtype=I32))
    bp0 = pick(start) + (jj - pick(blk_start)) * BM
    bend = pick(start + counts)
    bfirst = (jj == pick(blk_start)).astype(I32)
    return e_j, bp0.astype(I32), bend.astype(I32), bfirst, n_used.reshape(1).astype(I32)


def kernel(x, c, w_ada, b_ada, norm_mix_w, w_in, conv_w, conv_b, dt_bias, a_log, d_skip,
           ssd_norm_w, w_ssd_out, w_pool, pool_scale, w_pool_out, w_out, norm_ffn_w,
           w_router, b_router, w_gu, b_gu, w_down, b_down, norm_final_w):
    assert x.shape == (NB, S, D) and w_ada.shape[0] == 1

    mod = _adaln(c, w_ada[0], b_ada).reshape(NB, 6, D)

    w_in_b = w_in[0].astype(BF16)
    o1, o2, o3, o4 = D_SSD, D_SSD + D_XBC, D_SSD + D_XBC + N_HEADS, D_SSD + D_XBC + N_HEADS + D
    wz, wx, wp, wg = w_in_b[:, :o1], w_in_b[:, o1:o2], w_in_b[:, o3:o4], w_in_b[:, o4:]
    wdt = jnp.pad(w_in_b[:, o2:o3], ((0, 0), (0, LANES - N_HEADS)))
    z, xbc, dtr, up, gts = _inproj(x, mod, norm_mix_w, wz, wx, wdt, wp, wg)

    pad_h = lambda v: jnp.pad(v[0], (0, LANES - N_HEADS))
    dtb, alog = pad_h(dt_bias), pad_h(a_log)
    head_of_chan = jnp.arange(D_SSD, dtype=I32) // HEAD_DIM
    expand = (jnp.arange(LANES, dtype=I32)[:, None] == head_of_chan[None, :]).astype(F32)
    dskip = jnp.repeat(d_skip[0], HEAD_DIM).reshape(1, D_SSD)
    y = _ssd(xbc, z, dtr, conv_w[0], conv_b, dtb.reshape(1, LANES), alog.reshape(1, LANES),
             dtb.reshape(LANES, 1), alog.reshape(LANES, 1), dskip, ssd_norm_w, expand)

    x1, h2, eidx, grow = _merge(
        y, up, gts, x, mod, w_ssd_out[0].astype(BF16), w_pool[0].astype(BF16), pool_scale,
        w_pool_out[0].astype(BF16), w_out[0].astype(BF16), norm_ffn_w,
        w_router[0].T, b_router.reshape(N_EXP, 1))

    tok_of_row, row_of_slot, counts = _sort(eidx.reshape(SORT_ROWS, LANES))
    bexp, bp0, bend, bfirst, nused = _block_tables(counts[0, :N_EXP])

    rows3 = lambda v: v.reshape(-1, ROW_CHUNKS, LANES)
    lines2 = lambda v: v.reshape(-1, LANES)
    xs = _sc_gather_rows(rows3(h2), tok_of_row.reshape(N_SLOTS))
    yb = _experts(bexp, bp0, bend, bfirst, nused, lines2(xs), w_gu[0],
                  b_gu[0].reshape(N_EXP, 1, 2 * D), w_down[0], b_down[0].reshape(N_EXP, 1, D))
    yg = _sc_gather_rows(rows3(yb), row_of_slot.reshape(N_SLOTS))

    return _final(lines2(yg), grow, x1, mod, norm_final_w.reshape(1, D))
```

```python
import functools

import jax
import jax.numpy as jnp
from jax import lax
from jax.experimental import pallas as pl
from jax.experimental.pallas import tpu as pltpu
from jax.experimental.pallas import tpu_sc as plsc

F32 = jnp.float32
BF16 = jnp.bfloat16
I32 = jnp.int32
HIGHEST = lax.Precision.HIGHEST

D = 1024
NB = 4
S = 4096
T = NB * S
EPS = 1e-5
D_SSD = 2048
N_HEADS = 32
HEAD_DIM = 64
N_GROUPS = 4
D_STATE = 128
GROUP_W = D_SSD // N_GROUPS
D_XBC = D_SSD + 2 * N_GROUPS * D_STATE
CONV_K = 4
POOL_WINDOWS = (2, 4, 8, 16)
POOL_GW = 256
N_EXP = 32
TOP_K = 4
SWIGLU_LIMIT = 7.0
SWIGLU_ALPHA = 1.702

LANES = 128
SUBLANES = 8

CHUNK = 128
TM_PROJ = 512
TM_MERGE = 512
TM_FINAL = 512
BM = 256
N_SLOTS = T * TOP_K
N_BLOCKS = N_SLOTS // BM + N_EXP
HALO = 16
SORT_ROWS = N_SLOTS // LANES
ROW_CHUNKS = D // LANES

SC_CORES = 2
SC_SUBCORES = 16
SC_WORKERS = SC_CORES * SC_SUBCORES
SC_ROWS = 32

VMEM_LIMIT = 56 * 1024 * 1024


def _sigmoid(v):
    return 0.5 * jnp.tanh(0.5 * v) + 0.5


def _split3(v):
    hi = v.astype(BF16)
    rest = v - hi.astype(F32)
    mid = rest.astype(BF16)
    lo = (rest - mid.astype(F32)).astype(BF16)
    return hi, mid, lo


def _softplus(v):
    return jnp.maximum(v, 0.0) + jnp.log(1.0 + jnp.exp(-jnp.abs(v)))


def _resident(shape):
    nd = len(shape)
    return pl.BlockSpec(shape, lambda *_: (0,) * nd, pipeline_mode=pl.Buffered(1))


def _adaln_kernel(c_ref, w_ref, b_ref, o_ref):
    c = c_ref[...]
    act = c * _sigmoid(c)
    o_ref[...] = jnp.dot(act, w_ref[...], precision=HIGHEST,
                         preferred_element_type=F32) + b_ref[...]


def _adaln(c, w_ada, b_ada):
    tn = 1536
    return pl.pallas_call(
        _adaln_kernel,
        grid=(6 * D // tn,),
        in_specs=[pl.BlockSpec((NB, D), lambda j: (0, 0)),
                  pl.BlockSpec((D, tn), lambda j: (0, j)),
                  pl.BlockSpec((1, tn), lambda j: (0, j))],
        out_specs=pl.BlockSpec((NB, tn), lambda j: (0, j)),
        out_shape=jax.ShapeDtypeStruct((NB, 6 * D), F32),
        compiler_params=pltpu.CompilerParams(vmem_limit_bytes=VMEM_LIMIT),
        name="adaln",
    )(c, w_ada, b_ada)


def _inproj_kernel(x_ref, mod_ref, nw_ref, wz_ref, wx_ref, wdt_ref, wp_ref, wg_ref, cw_ref, cb_ref,
                   z_ref, xbc_ref, dt_ref, up_ref, g_ref, ubuf):
    tm = TM_PROJ

    @pl.when(pl.program_id(1) == 0)
    def _():
        ubuf[0:SUBLANES, :] = jnp.zeros((SUBLANES, D_XBC), F32)

    x = x_ref[0]
    ms = jnp.mean(x * x, axis=-1, keepdims=True)
    h = x * lax.rsqrt(ms + EPS) * nw_ref[...]
    h = h * (1.0 + mod_ref[0, 1:2, :]) + mod_ref[0, 0:1, :]
    hb = h.astype(BF16)
    z_ref[0] = jnp.dot(hb, wz_ref[...], preferred_element_type=F32).astype(BF16)

    ubuf[SUBLANES:SUBLANES + tm, :] = jnp.dot(hb, wx_ref[...], preferred_element_type=F32)
    conv = cb_ref[...] + cw_ref[0:1, :] * ubuf[pl.ds(SUBLANES - 3, tm), :]
    for k in range(1, CONV_K):
        conv = conv + cw_ref[k:k + 1, :] * ubuf[pl.ds(SUBLANES - 3 + k, tm), :]
    ubuf[0:SUBLANES, :] = ubuf[tm:tm + SUBLANES, :]
    xbc_ref[0] = (conv * _sigmoid(conv)).astype(BF16)

    dt_ref[0] = jnp.dot(hb, wdt_ref[...], preferred_element_type=F32)
    up_ref[0] = jnp.dot(hb, wp_ref[...], preferred_element_type=F32).astype(BF16)
    g_ref[0] = jnp.dot(hb, wg_ref[...], preferred_element_type=F32).astype(BF16)


def _inproj(x, mod, norm_w, wz, wx, wdt, wp, wg, conv_w, conv_b):
    tm = TM_PROJ
    tok = lambda n: pl.BlockSpec((1, tm, n), lambda b, i: (b, i, 0))
    return pl.pallas_call(
        _inproj_kernel,
        grid=(NB, S // tm),
        in_specs=[tok(D),
                  pl.BlockSpec((1, 6, D), lambda b, i: (b, 0, 0)),
                  _resident((1, D)),
                  _resident((D, D_SSD)), _resident((D, D_XBC)), _resident((D, LANES)),
                  _resident((D, D)), _resident((D, 2 * D)),
                  _resident((CONV_K, D_XBC)), _resident((1, D_XBC))],
        out_specs=[tok(D_SSD), tok(D_XBC), tok(LANES), tok(D), tok(2 * D)],
        out_shape=[jax.ShapeDtypeStruct((NB, S, D_SSD), BF16),
                   jax.ShapeDtypeStruct((NB, S, D_XBC), BF16),
                   jax.ShapeDtypeStruct((NB, S, LANES), F32),
                   jax.ShapeDtypeStruct((NB, S, D), BF16),
                   jax.ShapeDtypeStruct((NB, S, 2 * D), BF16)],
        scratch_shapes=[pltpu.VMEM((SUBLANES + tm, D_XBC), F32)],
        compiler_params=pltpu.CompilerParams(
            dimension_semantics=("arbitrary", "arbitrary"), vmem_limit_bytes=VMEM_LIMIT),
        name="inproj",
    )(x, mod, norm_w, wz, wx, wdt, wp, wg, conv_w, conv_b)


def _ssd_kernel(xbc_ref, z_ref, dtr_ref, dtb_row_ref, alog_row_ref,
                dtb_col_ref, alog_col_ref, dskip_ref, nw_ref, expand_ref,
                y_ref, state, ybuf):
    c = pl.program_id(1)

    @pl.when(c == 0)
    def _():
        state[...] = jnp.zeros_like(state)

    xc = xbc_ref[0]

    row = lax.broadcasted_iota(I32, (CHUNK, CHUNK), 0)
    col = lax.broadcasted_iota(I32, (CHUNK, CHUNK), 1)
    causal = row >= col
    tril = causal.astype(BF16)
    triu = (row <= col).astype(BF16)

    def ones_dot_cols(ones, v):
        parts = jnp.dot(ones, jnp.concatenate(_split3(v), axis=1), preferred_element_type=F32)
        n = v.shape[1]
        return parts[:, :n] + parts[:, n:2 * n] + parts[:, 2 * n:]

    def rows_dot_ones(v, ones):
        parts = jnp.dot(jnp.concatenate(_split3(v), axis=0), ones, preferred_element_type=F32)
        m = v.shape[0]
        return parts[:m] + parts[m:2 * m] + parts[2 * m:]

    dtr = dtr_ref[0]
    dt = _softplus(dtr + dtb_row_ref[...])
    a_cum = ones_dot_cols(tril, dt * (-jnp.exp(alog_row_ref[...])))
    dt_t = _softplus(dtr.T + dtb_col_ref[...])
    a_cum_t = rows_dot_ones(dt_t * (-jnp.exp(alog_col_ref[...])), triu)
    w_state_t = dt_t * jnp.exp(a_cum_t[:, CHUNK - 1:CHUNK] - a_cum_t)
    exp_a = jnp.exp(a_cum)
    chunk_decay = rows_dot_ones(exp_a[CHUNK - SUBLANES:CHUNK, :],
                                expand_ref[...])[SUBLANES - 1:SUBLANES, :]

    lane = lax.broadcasted_iota(I32, (CHUNK, LANES), 1)
    first_half = lane < HEAD_DIM

    def block_diag(v):
        v = v.astype(BF16)
        zero = jnp.zeros_like(v)
        return jnp.concatenate([jnp.where(first_half, v, zero),
                                jnp.where(first_half, zero, v)], axis=0)

    for g in range(N_GROUPS):
        b_g = xc[:, D_SSD + g * D_STATE:D_SSD + (g + 1) * D_STATE]
        c_g = xc[:, D_SSD + (N_GROUPS + g) * D_STATE:D_SSD + (N_GROUPS + g + 1) * D_STATE]
        cb = lax.dot_general(c_g, b_g, (((1,), (1,)), ((), ())),
                             preferred_element_type=F32)
        b_t = b_g.astype(F32).T
        for j in range(GROUP_W // LANES):
            lo = g * GROUP_W + j * LANES
            xp = xc[:, lo:lo + LANES]
            rhs_x = block_diag(xp)
            prev = state[g, :, j * LANES:(j + 1) * LANES]
            rhs_p = block_diag(prev)
            m_parts, o_parts, s_parts = [], [], []
            for hh in range(2):
                h = (lo // HEAD_DIM) + hh
                seg = a_cum[:, h:h + 1] - a_cum_t[h:h + 1, :]
                decay = jnp.exp(jnp.where(causal, seg, -jnp.inf))
                m_parts.append(cb * decay * dt_t[h:h + 1, :])
                o_parts.append(c_g * exp_a[:, h:h + 1])
                s_parts.append(b_t * w_state_t[h:h + 1, :])
            lhs_y = jnp.concatenate(m_parts, axis=1).astype(BF16)
            lhs_o = jnp.concatenate(o_parts, axis=1).astype(BF16)
            lhs_s = jnp.concatenate(s_parts, axis=1).astype(BF16)
            y_pair = (jnp.dot(lhs_y, rhs_x, preferred_element_type=F32)
                      + jnp.dot(lhs_o, rhs_p, preferred_element_type=F32))
            st_pair = jnp.dot(lhs_s, rhs_x, preferred_element_type=F32)
            state[g, :, j * LANES:(j + 1) * LANES] = (
                prev * chunk_decay[:, lo:lo + LANES] + st_pair)
            ybuf[:, lo:lo + LANES] = y_pair + dskip_ref[:, lo:lo + LANES] * xp

    zf = z_ref[0].astype(F32)
    yg = ybuf[...] * (zf * _sigmoid(zf))
    for g in range(N_GROUPS):
        blk = yg[:, g * GROUP_W:(g + 1) * GROUP_W]
        ms = jnp.mean(blk * blk, axis=-1, keepdims=True)
        y_ref[0, :, g * GROUP_W:(g + 1) * GROUP_W] = (
            blk * lax.rsqrt(ms + EPS) * nw_ref[:, g * GROUP_W:(g + 1) * GROUP_W]).astype(BF16)


def _ssd(xbc, z, dtr, dtb_row, alog_row, dtb_col, alog_col, dskip, norm_w, expand):
    tok = lambda n: pl.BlockSpec((1, CHUNK, n), lambda b, i: (b, i, 0))
    return pl.pallas_call(
        _ssd_kernel,
        grid=(NB, S // CHUNK),
        in_specs=[tok(D_XBC), tok(D_SSD), tok(LANES),
                  _resident((1, LANES)), _resident((1, LANES)),
                  _resident((LANES, 1)), _resident((LANES, 1)),
                  _resident((1, D_SSD)), _resident((1, D_SSD)),
                  _resident((LANES, D_SSD))],
        out_specs=tok(D_SSD),
        out_shape=jax.ShapeDtypeStruct((NB, S, D_SSD), BF16),
        scratch_shapes=[pltpu.VMEM((N_GROUPS, D_STATE, GROUP_W), F32),
                        pltpu.VMEM((CHUNK, D_SSD), F32)],
        compiler_params=pltpu.CompilerParams(
            dimension_semantics=("arbitrary", "arbitrary"), vmem_limit_bytes=VMEM_LIMIT),
        name="ssd",
    )(xbc, z, dtr, dtb_row, alog_row, dtb_col, alog_col, dskip, norm_w, expand)


def _merge_kernel(y_ref, up_ref, g_ref, x_ref, mod_ref, wso_ref, wpool_ref, pscale_ref,
                  wpo_ref, wout_ref, nw_ref, wr_ref, br_ref,
                  x1_ref, h2_ref, eidx_ref, grow_ref, ubuf):
    i = pl.program_id(1)
    tm = TM_MERGE

    @pl.when(i == 0)
    def _():
        ubuf[0:HALO, :] = jnp.zeros((HALO, D), F32)

    u = up_ref[0].astype(F32)
    ubuf[HALO:HALO + tm, :] = u
    ext = ubuf[...]
    ubuf[0:HALO, :] = ubuf[tm:tm + HALO, :]

    t_pos = i * tm + lax.broadcasted_iota(I32, (tm, 1), 0) + 1
    pooled = []
    for gi, w in enumerate(POOL_WINDOWS):
        acc = ext[:, gi * POOL_GW:(gi + 1) * POOL_GW]
        span = 1
        while span < w:
            acc = acc + pltpu.roll(acc, span, 0)
            span *= 2
        cnt = jnp.minimum(t_pos, w).astype(F32)
        p = acc[HALO:, :] / cnt - u[:, gi * POOL_GW:(gi + 1) * POOL_GW]
        pooled.append(jnp.dot(p.astype(BF16), wpool_ref[gi], preferred_element_type=F32))
    p = jnp.concatenate(pooled, axis=1) * pscale_ref[...]
    y_pool = jnp.dot(p.astype(BF16), wpo_ref[...], preferred_element_type=F32)
    y_ssd = jnp.dot(y_ref[0], wso_ref[...], preferred_element_type=F32)

    gt = _sigmoid(g_ref[0].astype(F32))
    merged = gt[:, :D] * y_ssd + gt[:, D:] * y_pool
    mix = jnp.dot(merged.astype(BF16), wout_ref[...], preferred_element_type=F32)
    x1 = x_ref[0] + mod_ref[0, 2:3, :] * mix
    x1_ref[0] = x1

    ms = jnp.mean(x1 * x1, axis=-1, keepdims=True)
    h2 = x1 * lax.rsqrt(ms + EPS) * nw_ref[...]
    h2 = h2 * (1.0 + mod_ref[0, 4:5, :]) + mod_ref[0, 3:4, :]
    for cc in range(ROW_CHUNKS):
        h2_ref[0, pl.ds(cc, tm, stride=ROW_CHUNKS), :] = h2[:, cc * LANES:(cc + 1) * LANES]

    logits = lax.dot_general(wr_ref[...], h2, (((1,), (1,)), ((), ())), precision=HIGHEST,
                             preferred_element_type=F32) + br_ref[...]
    e_iota = lax.broadcasted_iota(I32, (N_EXP, tm), 0)
    vals, idxs = [], []
    for _ in range(TOP_K):
        m = jnp.max(logits, axis=0, keepdims=True)
        idx = jnp.min(jnp.where(logits == m, e_iota, N_EXP), axis=0, keepdims=True)
        vals.append(m)
        idxs.append(idx)
        logits = jnp.where(e_iota == idx, -jnp.inf, logits)
    exps = [jnp.exp(v - vals[0]) for v in vals]
    denom = exps[0] + exps[1] + exps[2] + exps[3]
    eidx_ref[0] = jnp.concatenate(idxs, axis=0)
    gates_t = jnp.concatenate([e / denom for e in exps]
                              + [jnp.zeros((LANES - TOP_K, tm), F32)], axis=0)
    grow_ref[0] = gates_t.T


def _merge(y, up, gts, x, mod, wso, wpool, pscale, wpo, wout, nw, wr_t, br_col):
    tm = TM_MERGE
    tok = lambda n: pl.BlockSpec((1, tm, n), lambda b, i: (b, i, 0))
    return pl.pallas_call(
        _merge_kernel,
        grid=(NB, S // tm),
        in_specs=[tok(D_SSD), tok(D), tok(2 * D), tok(D),
                  pl.BlockSpec((1, 6, D), lambda b, i: (b, 0, 0)),
                  _resident((D_SSD, D)), _resident((len(POOL_WINDOWS), POOL_GW, POOL_GW)),
                  _resident((1, D)), _resident((D, D)), _resident((D, D)),
                  _resident((1, D)), _resident((N_EXP, D)), _resident((N_EXP, 1))],
        out_specs=[tok(D),
                   pl.BlockSpec((1, tm * ROW_CHUNKS, LANES), lambda b, i: (b, i, 0)),
                   pl.BlockSpec((1, TOP_K, tm), lambda b, i: (b, 0, i)),
                   tok(LANES)],
        out_shape=[jax.ShapeDtypeStruct((NB, S, D), F32),
                   jax.ShapeDtypeStruct((NB, S * ROW_CHUNKS, LANES), F32),
                   jax.ShapeDtypeStruct((NB, TOP_K, S), I32),
                   jax.ShapeDtypeStruct((NB, S, LANES), F32)],
        scratch_shapes=[pltpu.VMEM((HALO + tm, D), F32)],
        compiler_params=pltpu.CompilerParams(
            dimension_semantics=("arbitrary", "arbitrary"), vmem_limit_bytes=VMEM_LIMIT),
        name="merge",
    )(y, up, gts, x, mod, wso, wpool, pscale, wpo, wout, nw, wr_t, br_col)


def _sort_kernel(e_ref, tok_ref, dest_ref, counts_ref):
    row = lax.broadcasted_iota(I32, (SORT_ROWS, LANES), 0)
    lane = lax.broadcasted_iota(I32, (SORT_ROWS, LANES), 1)
    flat = row * LANES + lane
    s_bits = S.bit_length() - 1
    slot = (((flat >> s_bits) & (TOP_K - 1)) * T
            + (flat >> (s_bits + 2)) * S + (flat & (S - 1)))
    e = e_ref[...]
    keys = (e << 16) | slot

    lane_e = lax.broadcasted_iota(I32, (SUBLANES, LANES), 1)
    counts = jnp.zeros((SUBLANES, LANES), I32)
    for ex in range(N_EXP):
        n = jnp.sum((e == ex).astype(F32))
        counts = jnp.where(lane_e == ex, n.astype(I32), counts)
    counts_ref[...] = counts

    def partner(v, d):
        if d < LANES:
            down = pltpu.roll(v, d, 1)
            up = pltpu.roll(v, LANES - d, 1)
            return jnp.where((lane & d) != 0, down, up)
        r = d // LANES
        down = pltpu.roll(v, r, 0)
        up = pltpu.roll(v, SORT_ROWS - r, 0)
        return jnp.where((row & r) != 0, down, up)

    def bitonic_sort(keys):
        k = 2
        while k <= N_SLOTS:
            d = k // 2
            while d >= 1:
                other = partner(keys, d)
                is_low = (flat & d) == 0
                ascending = (flat & k) == 0
                take_min = is_low == ascending
                keys = jnp.where(take_min, jnp.minimum(keys, other), jnp.maximum(keys, other))
                d //= 2
            k *= 2
        return keys

    by_expert = bitonic_sort(keys) & (N_SLOTS - 1)
    tok_ref[...] = by_expert & (T - 1)
    by_slot = bitonic_sort(((by_expert - N_SLOTS // 2) << 16) | flat)
    dest_ref[...] = by_slot & (N_SLOTS - 1)


def _sort(e2d):
    return pl.pallas_call(
        _sort_kernel,
        out_shape=[jax.ShapeDtypeStruct((SORT_ROWS, LANES), I32),
                   jax.ShapeDtypeStruct((SORT_ROWS, LANES), I32),
                   jax.ShapeDtypeStruct((SUBLANES, LANES), I32)],
        compiler_params=pltpu.CompilerParams(vmem_limit_bytes=VMEM_LIMIT),
        name="route_sort",
    )(e2d)


def _sc_gather_rows(table, idx):
    n = idx.shape[0]
    per_worker = n // SC_WORKERS
    n_chunks = per_worker // SC_ROWS
    assert per_worker * SC_WORKERS == n and n_chunks * SC_ROWS == per_worker and n_chunks % 2 == 0
    mesh = plsc.VectorSubcoreMesh(core_axis_name="core", subcore_axis_name="subcore",
                                  num_cores=SC_CORES, num_subcores=SC_SUBCORES)
    row = (SC_ROWS, ROW_CHUNKS, LANES)

    def body(table_hbm, idx_hbm, out_hbm, iv0, iv1, r0, r1, is0, is1, gs0, gs1):
        idx_v, rows, idx_sem, row_sem = (iv0, iv1), (r0, r1), (is0, is1), (gs0, gs1)
        worker = lax.axis_index("subcore") * SC_CORES + lax.axis_index("core")
        base = worker * per_worker

        def idx_copy(i, b):
            return pltpu.make_async_copy(idx_hbm.at[pl.ds(base + i * SC_ROWS, SC_ROWS)],
                                         idx_v[b], idx_sem[b])

        def row_gather(b):
            return pltpu.make_async_copy(table_hbm.at[idx_v[b]], rows[b], row_sem[b])

        idx_copy(0, 0).start()
        idx_copy(0, 0).wait()
        row_gather(0).start()
        idx_copy(1, 1).start()

        @pl.loop(0, n_chunks // 2)
        def _(pair):
            for b in (0, 1):
                i = 2 * pair + b
                row_gather(b).wait()

                @pl.when(i + 1 < n_chunks)
                def _():
                    idx_copy(i + 1, 1 - b).wait()
                    row_gather(1 - b).start()

                @pl.when(i + 2 < n_chunks)
                def _():
                    idx_copy(i + 2, b).start()

                pltpu.sync_copy(rows[b], out_hbm.at[pl.ds(base + i * SC_ROWS, SC_ROWS)])

    return pl.kernel(
        body, mesh=mesh,
        out_type=jax.ShapeDtypeStruct((n, ROW_CHUNKS, LANES), F32),
        scratch_types=[pltpu.VMEM((SC_ROWS,), I32), pltpu.VMEM((SC_ROWS,), I32),
                       pltpu.VMEM(row, F32), pltpu.VMEM(row, F32),
                       pltpu.SemaphoreType.DMA, pltpu.SemaphoreType.DMA,
                       pltpu.SemaphoreType.DMA, pltpu.SemaphoreType.DMA],
    )(table, idx)


def _expert_kernel(bexp_ref, bp0_ref, bend_ref, bfirst_ref, nused_ref,
                   xs_hbm, wgu_ref, bgu_ref, wd_ref, bd_ref, yb_hbm,
                   xbuf, obuf, wgu_bf, wd_bf, in_sem, out_sem):
    j = pl.program_id(0)
    n_used = nused_ref[0]

    def n_rows(jj):
        return jnp.clip(bend_ref[jj] - bp0_ref[jj], 0, BM)

    def slab_in(jj, buf):
        n = pl.multiple_of(n_rows(jj) * ROW_CHUNKS, ROW_CHUNKS)
        p0 = pl.multiple_of(bp0_ref[jj] * ROW_CHUNKS, ROW_CHUNKS)
        return pltpu.make_async_copy(xs_hbm.at[pl.ds(p0, n)],
                                     xbuf.at[buf, pl.ds(0, n)], in_sem.at[buf])

    def slab_out(jj, buf):
        n = pl.multiple_of(n_rows(jj) * ROW_CHUNKS, ROW_CHUNKS)
        p0 = pl.multiple_of(bp0_ref[jj] * ROW_CHUNKS, ROW_CHUNKS)
        return pltpu.make_async_copy(obuf.at[buf, pl.ds(0, n)],
                                     yb_hbm.at[pl.ds(p0, n)], out_sem.at[buf])

    @pl.when(j == 0)
    def _():
        xbuf[...] = jnp.zeros_like(xbuf)
        slab_in(0, 0).start()

    @pl.when(j < n_used)
    def _():
        buf = j % 2
        slab_in(j, buf).wait()

        @pl.when(j + 1 < n_used)
        def _():
            slab_in(j + 1, 1 - buf).start()

        @pl.when(bfirst_ref[j] == 1)
        def _():
            wgu_bf[...] = wgu_ref[0].astype(BF16)
            wd_bf[...] = wd_ref[0].astype(BF16)

        x = jnp.concatenate([xbuf[buf, pl.ds(cc, BM, stride=ROW_CHUNKS), :]
                             for cc in range(ROW_CHUNKS)], axis=1).astype(BF16)
        gu = jnp.dot(x, wgu_bf[...], preferred_element_type=F32) + bgu_ref[0]
        glu = jnp.minimum(gu[:, :D], SWIGLU_LIMIT)
        lin = jnp.clip(gu[:, D:], -SWIGLU_LIMIT, SWIGLU_LIMIT)
        act = glu * _sigmoid(SWIGLU_ALPHA * glu) * (lin + 1.0)
        y = jnp.dot(act.astype(BF16), wd_bf[...], preferred_element_type=F32) + bd_ref[0]

        @pl.when(j >= 2)
        def _():
            slab_out(j - 2, buf).wait()

        for cc in range(ROW_CHUNKS):
            obuf[buf, pl.ds(cc, BM, stride=ROW_CHUNKS), :] = y[:, cc * LANES:(cc + 1) * LANES]
        slab_out(j, buf).start()

        @pl.when(j == n_used - 1)
        def _():
            slab_out(j, buf).wait()

            @pl.when(j >= 1)
            def _():
                slab_out(j - 1, 1 - buf).wait()


def _experts(bexp, bp0, bend, bfirst, nused, xs, w_gu, b_gu, w_down, b_down):
    grid_spec = pltpu.PrefetchScalarGridSpec(
        num_scalar_prefetch=5,
        grid=(N_BLOCKS,),
        in_specs=[pl.BlockSpec(memory_space=pl.ANY),
                  pl.BlockSpec((1, D, 2 * D), lambda j, be, *_: (be[j], 0, 0)),
                  pl.BlockSpec((1, 1, 2 * D), lambda j, be, *_: (be[j], 0, 0)),
                  pl.BlockSpec((1, D, D), lambda j, be, *_: (be[j], 0, 0)),
                  pl.BlockSpec((1, 1, D), lambda j, be, *_: (be[j], 0, 0))],
        out_specs=pl.BlockSpec(memory_space=pl.ANY),
        scratch_shapes=[pltpu.VMEM((2, BM * ROW_CHUNKS, LANES), F32),
                        pltpu.VMEM((2, BM * ROW_CHUNKS, LANES), F32),
                        pltpu.VMEM((D, 2 * D), BF16),
                        pltpu.VMEM((D, D), BF16),
                        pltpu.SemaphoreType.DMA((2,)),
                        pltpu.SemaphoreType.DMA((2,))],
    )
    return pl.pallas_call(
        _expert_kernel,
        grid_spec=grid_spec,
        out_shape=jax.ShapeDtypeStruct((N_SLOTS * ROW_CHUNKS, LANES), F32),
        compiler_params=pltpu.CompilerParams(
            dimension_semantics=("arbitrary",), vmem_limit_bytes=VMEM_LIMIT),
        name="experts",
    )(bexp, bp0, bend, bfirst, nused, xs, w_gu, b_gu, w_down, b_down)


def _final_kernel(y0_ref, y1_ref, y2_ref, y3_ref, grow_ref, x1_ref, mod_ref, nw_ref, o_ref):
    gr = grow_ref[0]
    gates = [gr[:, k:k + 1] for k in range(TOP_K)]
    y = jnp.concatenate(
        [sum(gates[k] * ref[pl.ds(cc, TM_FINAL, stride=ROW_CHUNKS), :]
             for k, ref in enumerate((y0_ref, y1_ref, y2_ref, y3_ref)))
         for cc in range(ROW_CHUNKS)], axis=1)
    x2 = x1_ref[0] + mod_ref[0, 5:6, :] * y
    ms = jnp.mean(x2 * x2, axis=-1, keepdims=True)
    o_ref[0] = x2 * lax.rsqrt(ms + EPS) * nw_ref[...]


def _final(yg, grow, x1, mod, nw):
    tm = TM_FINAL
    per_b = S // tm
    per_k = T // tm

    def yspec(k):
        return pl.BlockSpec((tm * ROW_CHUNKS, LANES),
                            lambda b, i: (k * per_k + b * per_b + i, 0))

    tok = lambda n: pl.BlockSpec((1, tm, n), lambda b, i: (b, i, 0))
    return pl.pallas_call(
        _final_kernel,
        grid=(NB, per_b),
        in_specs=[yspec(0), yspec(1), yspec(2), yspec(3), tok(LANES), tok(D),
                  pl.BlockSpec((1, 6, D), lambda b, i: (b, 0, 0)),
                  _resident((1, D))],
        out_specs=tok(D),
        out_shape=jax.ShapeDtypeStruct((NB, S, D), F32),
        compiler_params=pltpu.CompilerParams(
            dimension_semantics=("arbitrary", "arbitrary"), vmem_limit_bytes=VMEM_LIMIT),
        name="final",
    )(yg, yg, yg, yg, grow, x1, mod, nw)


def _block_tables(counts):
    nblk = (counts + BM - 1) // BM
    blk_cum = jnp.cumsum(nblk)
    n_used = blk_cum[-1]
    blk_start = blk_cum - nblk
    start = jnp.cumsum(counts) - counts
    jj = jnp.arange(N_BLOCKS, dtype=I32)
    jc = jnp.minimum(jj, n_used - 1)[:, None]
    owner = ((blk_start[None, :] <= jc) & (jc < blk_cum[None, :])).astype(I32)
    pick = lambda v: jnp.sum(owner * v[None, :], axis=1)
    e_j = pick(jnp.arange(N_EXP, dtype=I32))
    bp0 = pick(start) + (jj - pick(blk_start)) * BM
    bend = pick(start + counts)
    bfirst = (jj == pick(blk_start)).astype(I32)
    return e_j, bp0.astype(I32), bend.astype(I32), bfirst, n_used.reshape(1).astype(I32)


def kernel(x, c, w_ada, b_ada, norm_mix_w, w_in, conv_w, conv_b, dt_bias, a_log, d_skip,
           ssd_norm_w, w_ssd_out, w_pool, pool_scale, w_pool_out, w_out, norm_ffn_w,
           w_router, b_router, w_gu, b_gu, w_down, b_down, norm_final_w):
    assert x.shape == (NB, S, D) and w_ada.shape[0] == 1

    mod = _adaln(c, w_ada[0], b_ada).reshape(NB, 6, D)

    w_in_b = w_in[0].astype(BF16)
    o1, o2, o3, o4 = D_SSD, D_SSD + D_XBC, D_SSD + D_XBC + N_HEADS, D_SSD + D_XBC + N_HEADS + D
    wz, wx, wp, wg = w_in_b[:, :o1], w_in_b[:, o1:o2], w_in_b[:, o3:o4], w_in_b[:, o4:]
    wdt = jnp.pad(w_in_b[:, o2:o3], ((0, 0), (0, LANES - N_HEADS)))
    z, xbc, dtr, up, gts = _inproj(x, mod, norm_mix_w, wz, wx, wdt, wp, wg, conv_w[0], conv_b)

    pad_h = lambda v: jnp.pad(v[0], (0, LANES - N_HEADS))
    dtb, alog = pad_h(dt_bias), pad_h(a_log)
    head_of_chan = jnp.arange(D_SSD, dtype=I32) // HEAD_DIM
    expand = (jnp.arange(LANES, dtype=I32)[:, None] == head_of_chan[None, :]).astype(BF16)
    dskip = jnp.repeat(d_skip[0], HEAD_DIM).reshape(1, D_SSD)
    y = _ssd(xbc, z, dtr, dtb.reshape(1, LANES), alog.reshape(1, LANES),
             dtb.reshape(LANES, 1), alog.reshape(LANES, 1), dskip, ssd_norm_w, expand)

    x1, h2, eidx, grow = _merge(
        y, up, gts, x, mod, w_ssd_out[0].astype(BF16), w_pool[0].astype(BF16), pool_scale,
        w_pool_out[0].astype(BF16), w_out[0].astype(BF16), norm_ffn_w,
        w_router[0].T, b_router.reshape(N_EXP, 1))

    tok_of_row, row_of_slot, counts = _sort(eidx.reshape(SORT_ROWS, LANES))
    bexp, bp0, bend, bfirst, nused = _block_tables(counts[0, :N_EXP])

    rows3 = lambda v: v.reshape(-1, ROW_CHUNKS, LANES)
    lines2 = lambda v: v.reshape(-1, LANES)
    xs = _sc_gather_rows(rows3(h2), tok_of_row.reshape(N_SLOTS))
    yb = _experts(bexp, bp0, bend, bfirst, nused, lines2(xs), w_gu[0],
                  b_gu[0].reshape(N_EXP, 1, 2 * D), w_down[0], b_down[0].reshape(N_EXP, 1, D))
    yg = _sc_gather_rows(rows3(yb), row_of_slot.reshape(N_SLOTS))

    return _final(lines2(yg), grow, x1, mod, norm_final_w.reshape(1, D))
```

```python
import functools

import jax
import jax.numpy as jnp
from jax import lax
from jax.experimental import pallas as pl
from jax.experimental.pallas import tpu as pltpu
from jax.experimental.pallas import tpu_sc as plsc

F32 = jnp.float32
BF16 = jnp.bfloat16
I32 = jnp.int32
HIGHEST = lax.Precision.HIGHEST

D = 1024
NB = 4
S = 4096
EPS = 1e-5
D_SSD = 2048
N_HEADS = 32
HEAD_DIM = 64
N_GROUPS = 4
D_STATE = 128
GROUP_W = D_SSD // N_GROUPS
D_XBC = D_SSD + 2 * N_GROUPS * D_STATE
CONV_K = 4
POOL_WINDOWS = (2, 4, 8, 16)
POOL_GW = 256
N_EXP = 32
TOP_K = 4
SWIGLU_LIMIT = 7.0
SWIGLU_ALPHA = 1.702

LANES = 128
SUBLANES = 8

CHUNK = 128
TM_PROJ = 512
TM_MERGE = 512
TM_FINAL = 512
BM = 256
GB = 2
N_SLOTS = GB * S * TOP_K
N_BLOCKS = N_SLOTS // BM + N_EXP
SLOT_MASK = (1 << 16) - 1
HALO = 16
SORT_ROWS = N_SLOTS // LANES
ROW_CHUNKS = D // LANES

SC_CORES = 2
SC_SUBCORES = 16
SC_WORKERS = SC_CORES * SC_SUBCORES
SC_ROWS = 32

VMEM_LIMIT = 56 * 1024 * 1024


def _sigmoid(v):
    return 0.5 * jnp.tanh(0.5 * v) + 0.5


def _split3(v):
    hi = v.astype(BF16)
    rest = v - hi.astype(F32)
    mid = rest.astype(BF16)
    lo = (rest - mid.astype(F32)).astype(BF16)
    return hi, mid, lo


def _softplus(v):
    return jnp.maximum(v, 0.0) + jnp.log(1.0 + jnp.exp(-jnp.abs(v)))


def _resident(shape):
    nd = len(shape)
    return pl.BlockSpec(shape, lambda *_: (0,) * nd, pipeline_mode=pl.Buffered(1))


def _adaln_kernel(c_ref, w_ref, b_ref, o_ref):
    c = c_ref[...]
    act = c * _sigmoid(c)
    o_ref[...] = jnp.dot(act, w_ref[...], precision=HIGHEST,
                         preferred_element_type=F32) + b_ref[...]


def _adaln(c, w_ada, b_ada):
    tn = 1536
    return pl.pallas_call(
        _adaln_kernel,
        grid=(6 * D // tn,),
        in_specs=[pl.BlockSpec((NB, D), lambda j: (0, 0)),
                  pl.BlockSpec((D, tn), lambda j: (0, j)),
                  pl.BlockSpec((1, tn), lambda j: (0, j))],
        out_specs=pl.BlockSpec((NB, tn), lambda j: (0, j)),
        out_shape=jax.ShapeDtypeStruct((NB, 6 * D), F32),
        compiler_params=pltpu.CompilerParams(vmem_limit_bytes=VMEM_LIMIT),
        name="adaln",
    )(c, w_ada, b_ada)


def _inproj_kernel(x_ref, mod_ref, nw_ref, wz_ref, wx_ref, wdt_ref, wp_ref, wg_ref, cw_ref, cb_ref,
                   z_ref, xbc_ref, dt_ref, up_ref, g_ref, ubuf):
    tm = TM_PROJ

    @pl.when(pl.program_id(1) == 0)
    def _():
        ubuf[0:SUBLANES, :] = jnp.zeros((SUBLANES, D_XBC), F32)

    x = x_ref[0]
    ms = jnp.mean(x * x, axis=-1, keepdims=True)
    h = x * lax.rsqrt(ms + EPS) * nw_ref[...]
    h = h * (1.0 + mod_ref[0, 1:2, :]) + mod_ref[0, 0:1, :]
    hb = h.astype(BF16)
    z_ref[0] = jnp.dot(hb, wz_ref[...], preferred_element_type=F32).astype(BF16)

    ubuf[SUBLANES:SUBLANES + tm, :] = jnp.dot(hb, wx_ref[...], preferred_element_type=F32)
    conv = cb_ref[...] + cw_ref[0:1, :] * ubuf[pl.ds(SUBLANES - 3, tm), :]
    for k in range(1, CONV_K):
        conv = conv + cw_ref[k:k + 1, :] * ubuf[pl.ds(SUBLANES - 3 + k, tm), :]
    ubuf[0:SUBLANES, :] = ubuf[tm:tm + SUBLANES, :]
    xbc_ref[0] = (conv * _sigmoid(conv)).astype(BF16)

    dt_ref[0] = jnp.dot(hb, wdt_ref[...], preferred_element_type=F32)
    up_ref[0] = jnp.dot(hb, wp_ref[...], preferred_element_type=F32).astype(BF16)
    g_ref[0] = jnp.dot(hb, wg_ref[...], preferred_element_type=F32).astype(BF16)


def _inproj(x, mod, norm_w, wz, wx, wdt, wp, wg, conv_w, conv_b):
    tm = TM_PROJ
    tok = lambda n: pl.BlockSpec((1, tm, n), lambda b, i: (b, i, 0))
    return pl.pallas_call(
        _inproj_kernel,
        grid=(NB, S // tm),
        in_specs=[tok(D),
                  pl.BlockSpec((1, 6, D), lambda b, i: (b, 0, 0)),
                  _resident((1, D)),
                  _resident((D, D_SSD)), _resident((D, D_XBC)), _resident((D, LANES)),
                  _resident((D, D)), _resident((D, 2 * D)),
                  _resident((CONV_K, D_XBC)), _resident((1, D_XBC))],
        out_specs=[tok(D_SSD), tok(D_XBC), tok(LANES), tok(D), tok(2 * D)],
        out_shape=[jax.ShapeDtypeStruct((NB, S, D_SSD), BF16),
                   jax.ShapeDtypeStruct((NB, S, D_XBC), BF16),
                   jax.ShapeDtypeStruct((NB, S, LANES), F32),
                   jax.ShapeDtypeStruct((NB, S, D), BF16),
                   jax.ShapeDtypeStruct((NB, S, 2 * D), BF16)],
        scratch_shapes=[pltpu.VMEM((SUBLANES + tm, D_XBC), F32)],
        compiler_params=pltpu.CompilerParams(
            dimension_semantics=("arbitrary", "arbitrary"), vmem_limit_bytes=VMEM_LIMIT),
        name="inproj",
    )(x, mod, norm_w, wz, wx, wdt, wp, wg, conv_w, conv_b)


def _ssd_kernel(xbc_ref, z_ref, dtr_ref, dtb_row_ref, alog_row_ref,
                dtb_col_ref, alog_col_ref, dskip_ref, nw_ref, expand_ref,
                y_ref, state, ybuf):
    c = pl.program_id(1)

    @pl.when(c == 0)
    def _():
        state[...] = jnp.zeros_like(state)

    xc = xbc_ref[0]

    row = lax.broadcasted_iota(I32, (CHUNK, CHUNK), 0)
    col = lax.broadcasted_iota(I32, (CHUNK, CHUNK), 1)
    causal = row >= col
    tril = causal.astype(BF16)
    triu = (row <= col).astype(BF16)

    def ones_dot_cols(ones, v):
        parts = jnp.dot(ones, jnp.concatenate(_split3(v), axis=1), preferred_element_type=F32)
        n = v.shape[1]
        return parts[:, :n] + parts[:, n:2 * n] + parts[:, 2 * n:]

    def rows_dot_ones(v, ones):
        parts = jnp.dot(jnp.concatenate(_split3(v), axis=0), ones, preferred_element_type=F32)
        m = v.shape[0]
        return parts[:m] + parts[m:2 * m] + parts[2 * m:]

    dtr = dtr_ref[0]
    dt = _softplus(dtr + dtb_row_ref[...])
    a_cum = ones_dot_cols(tril, dt * (-jnp.exp(alog_row_ref[...])))
    dt_t = _softplus(dtr.T + dtb_col_ref[...])
    a_cum_t = rows_dot_ones(dt_t * (-jnp.exp(alog_col_ref[...])), triu)
    w_state_t = dt_t * jnp.exp(a_cum_t[:, CHUNK - 1:CHUNK] - a_cum_t)
    exp_a = jnp.exp(a_cum)
    chunk_decay = rows_dot_ones(exp_a[CHUNK - SUBLANES:CHUNK, :],
                                expand_ref[...])[SUBLANES - 1:SUBLANES, :]

    lane = lax.broadcasted_iota(I32, (CHUNK, LANES), 1)
    first_half = lane < HEAD_DIM

    def block_diag(v):
        v = v.astype(BF16)
        zero = jnp.zeros_like(v)
        return jnp.concatenate([jnp.where(first_half, v, zero),
                                jnp.where(first_half, zero, v)], axis=0)

    for g in range(N_GROUPS):
        b_g = xc[:, D_SSD + g * D_STATE:D_SSD + (g + 1) * D_STATE]
        c_g = xc[:, D_SSD + (N_GROUPS + g) * D_STATE:D_SSD + (N_GROUPS + g + 1) * D_STATE]
        cb = lax.dot_general(c_g, b_g, (((1,), (1,)), ((), ())),
                             preferred_element_type=F32)
        b_t = b_g.astype(F32).T
        for j in range(GROUP_W // LANES):
            lo = g * GROUP_W + j * LANES
            xp = xc[:, lo:lo + LANES]
            rhs_x = block_diag(xp)
            prev = state[g, :, j * LANES:(j + 1) * LANES]
            rhs_p = block_diag(prev)
            m_parts, o_parts, s_parts = [], [], []
            for hh in range(2):
                h = (lo // HEAD_DIM) + hh
                seg = a_cum[:, h:h + 1] - a_cum_t[h:h + 1, :]
                decay = jnp.exp(jnp.where(causal, seg, -jnp.inf))
                m_parts.append(cb * decay * dt_t[h:h + 1, :])
                o_parts.append(c_g * exp_a[:, h:h + 1])
                s_parts.append(b_t * w_state_t[h:h + 1, :])
            lhs_y = jnp.concatenate(m_parts, axis=1).astype(BF16)
            lhs_o = jnp.concatenate(o_parts, axis=1).astype(BF16)
            lhs_s = jnp.concatenate(s_parts, axis=1).astype(BF16)
            y_pair = (jnp.dot(lhs_y, rhs_x, preferred_element_type=F32)
                      + jnp.dot(lhs_o, rhs_p, preferred_element_type=F32))
            st_pair = jnp.dot(lhs_s, rhs_x, preferred_element_type=F32)
            state[g, :, j * LANES:(j + 1) * LANES] = (
                prev * chunk_decay[:, lo:lo + LANES] + st_pair)
            ybuf[:, lo:lo + LANES] = y_pair + dskip_ref[:, lo:lo + LANES] * xp

    zf = z_ref[0].astype(F32)
    yg = ybuf[...] * (zf * _sigmoid(zf))
    for g in range(N_GROUPS):
        blk = yg[:, g * GROUP_W:(g + 1) * GROUP_W]
        ms = jnp.mean(blk * blk, axis=-1, keepdims=True)
        y_ref[0, :, g * GROUP_W:(g + 1) * GROUP_W] = (
            blk * lax.rsqrt(ms + EPS) * nw_ref[:, g * GROUP_W:(g + 1) * GROUP_W]).astype(BF16)


def _ssd(xbc, z, dtr, dtb_row, alog_row, dtb_col, alog_col, dskip, norm_w, expand):
    tok = lambda n: pl.BlockSpec((1, CHUNK, n), lambda b, i: (b, i, 0))
    return pl.pallas_call(
        _ssd_kernel,
        grid=(NB, S // CHUNK),
        in_specs=[tok(D_XBC), tok(D_SSD), tok(LANES),
                  _resident((1, LANES)), _resident((1, LANES)),
                  _resident((LANES, 1)), _resident((LANES, 1)),
                  _resident((1, D_SSD)), _resident((1, D_SSD)),
                  _resident((LANES, D_SSD))],
        out_specs=tok(D_SSD),
        out_shape=jax.ShapeDtypeStruct((NB, S, D_SSD), BF16),
        scratch_shapes=[pltpu.VMEM((N_GROUPS, D_STATE, GROUP_W), F32),
                        pltpu.VMEM((CHUNK, D_SSD), F32)],
        compiler_params=pltpu.CompilerParams(
            dimension_semantics=("arbitrary", "arbitrary"), vmem_limit_bytes=VMEM_LIMIT),
        name="ssd",
    )(xbc, z, dtr, dtb_row, alog_row, dtb_col, alog_col, dskip, norm_w, expand)


def _merge_kernel(y_ref, up_ref, g_ref, x_ref, mod_ref, wso_ref, wpool_ref, pscale_ref,
                  wpo_ref, wout_ref, nw_ref, wr_ref, br_ref,
                  x1_ref, h2_ref, eidx_ref, grow_ref, ubuf):
    i = pl.program_id(1)
    tm = TM_MERGE

    @pl.when(i == 0)
    def _():
        ubuf[0:HALO, :] = jnp.zeros((HALO, D), F32)

    u = up_ref[0].astype(F32)
    ubuf[HALO:HALO + tm, :] = u
    ext = ubuf[...]
    ubuf[0:HALO, :] = ubuf[tm:tm + HALO, :]

    t_pos = i * tm + lax.broadcasted_iota(I32, (tm, 1), 0) + 1
    pooled = []
    for gi, w in enumerate(POOL_WINDOWS):
        acc = ext[:, gi * POOL_GW:(gi + 1) * POOL_GW]
        span = 1
        while span < w:
            acc = acc + pltpu.roll(acc, span, 0)
            span *= 2
        cnt = jnp.minimum(t_pos, w).astype(F32)
        p = acc[HALO:, :] / cnt - u[:, gi * POOL_GW:(gi + 1) * POOL_GW]
        pooled.append(jnp.dot(p.astype(BF16), wpool_ref[gi], preferred_element_type=F32))
    p = jnp.concatenate(pooled, axis=1) * pscale_ref[...]
    y_pool = jnp.dot(p.astype(BF16), wpo_ref[...], preferred_element_type=F32)
    y_ssd = jnp.dot(y_ref[0], wso_ref[...], preferred_element_type=F32)

    gt = _sigmoid(g_ref[0].astype(F32))
    merged = gt[:, :D] * y_ssd + gt[:, D:] * y_pool
    mix = jnp.dot(merged.astype(BF16), wout_ref[...], preferred_element_type=F32)
    x1 = x_ref[0] + mod_ref[0, 2:3, :] * mix
    x1_ref[0] = x1

    ms = jnp.mean(x1 * x1, axis=-1, keepdims=True)
    h2 = x1 * lax.rsqrt(ms + EPS) * nw_ref[...]
    h2 = h2 * (1.0 + mod_ref[0, 4:5, :]) + mod_ref[0, 3:4, :]
    for cc in range(ROW_CHUNKS):
        h2_ref[0, pl.ds(cc, tm, stride=ROW_CHUNKS), :] = h2[:, cc * LANES:(cc + 1) * LANES]

    logits = lax.dot_general(wr_ref[...], h2, (((1,), (1,)), ((), ())), precision=HIGHEST,
                             preferred_element_type=F32) + br_ref[...]
    e_iota = lax.broadcasted_iota(I32, (N_EXP, tm), 0)
    vals, idxs = [], []
    for _ in range(TOP_K):
        m = jnp.max(logits, axis=0, keepdims=True)
        idx = jnp.min(jnp.where(logits == m, e_iota, N_EXP), axis=0, keepdims=True)
        vals.append(m)
        idxs.append(idx)
        logits = jnp.where(e_iota == idx, -jnp.inf, logits)
    exps = [jnp.exp(v - vals[0]) for v in vals]
    denom = exps[0] + exps[1] + exps[2] + exps[3]
    eidx_ref[0] = jnp.concatenate(idxs, axis=0)
    gates_t = jnp.concatenate([e / denom for e in exps]
                              + [jnp.zeros((LANES - TOP_K, tm), F32)], axis=0)
    grow_ref[0] = gates_t.T


def _merge(b0, y, up, gts, x, mod, wso, wpool, pscale, wpo, wout, nw, wr_t, br_col):
    tm = TM_MERGE
    tok_in = lambda n: pl.BlockSpec((1, tm, n), lambda b, i: (b + b0, i, 0))
    tok = lambda n: pl.BlockSpec((1, tm, n), lambda b, i: (b, i, 0))
    return pl.pallas_call(
        _merge_kernel,
        grid=(GB, S // tm),
        in_specs=[tok_in(D_SSD), tok_in(D), tok_in(2 * D), tok_in(D),
                  pl.BlockSpec((1, 6, D), lambda b, i: (b + b0, 0, 0)),
                  _resident((D_SSD, D)), _resident((len(POOL_WINDOWS), POOL_GW, POOL_GW)),
                  _resident((1, D)), _resident((D, D)), _resident((D, D)),
                  _resident((1, D)), _resident((N_EXP, D)), _resident((N_EXP, 1))],
        out_specs=[tok(D),
                   pl.BlockSpec((1, tm * ROW_CHUNKS, LANES), lambda b, i: (b, i, 0)),
                   pl.BlockSpec((1, TOP_K, tm), lambda b, i: (b, 0, i)),
                   tok(LANES)],
        out_shape=[jax.ShapeDtypeStruct((GB, S, D), F32),
                   jax.ShapeDtypeStruct((GB, S * ROW_CHUNKS, LANES), F32),
                   jax.ShapeDtypeStruct((GB, TOP_K, S), I32),
                   jax.ShapeDtypeStruct((GB, S, LANES), F32)],
        scratch_shapes=[pltpu.VMEM((HALO + tm, D), F32)],
        compiler_params=pltpu.CompilerParams(
            dimension_semantics=("arbitrary", "arbitrary"), vmem_limit_bytes=VMEM_LIMIT),
        name="merge",
    )(y, up, gts, x, mod, wso, wpool, pscale, wpo, wout, nw, wr_t, br_col)


def _sort_kernel(e_ref, tok_ref, dest_ref, counts_ref):
    row = lax.broadcasted_iota(I32, (SORT_ROWS, LANES), 0)
    lane = lax.broadcasted_iota(I32, (SORT_ROWS, LANES), 1)
    flat = row * LANES + lane
    s_bits = S.bit_length() - 1
    k_bits = TOP_K.bit_length() - 1
    e = e_ref[...]
    keys = (e << 16) | flat

    lane_e = lax.broadcasted_iota(I32, (SUBLANES, LANES), 1)
    counts = jnp.zeros((SUBLANES, LANES), I32)
    for ex in range(N_EXP):
        n = jnp.sum((e == ex).astype(F32))
        counts = jnp.where(lane_e == ex, n.astype(I32), counts)
    counts_ref[...] = counts

    def partner(v, d):
        if d < LANES:
            down = pltpu.roll(v, d, 1)
            up = pltpu.roll(v, LANES - d, 1)
            return jnp.where((lane & d) != 0, down, up)
        r = d // LANES
        down = pltpu.roll(v, r, 0)
        up = pltpu.roll(v, SORT_ROWS - r, 0)
        return jnp.where((row & r) != 0, down, up)

    def bitonic_sort(keys):
        k = 2
        while k <= N_SLOTS:
            d = k // 2
            while d >= 1:
                other = partner(keys, d)
                is_low = (flat & d) == 0
                ascending = (flat & k) == 0
                take_min = is_low == ascending
                keys = jnp.where(take_min, jnp.minimum(keys, other), jnp.maximum(keys, other))
                d //= 2
            k *= 2
        return keys

    by_expert = bitonic_sort(keys) & SLOT_MASK
    tok_ref[...] = ((by_expert >> (s_bits + k_bits)) << s_bits) | (by_expert & (S - 1))
    by_slot = bitonic_sort(((by_expert - N_SLOTS // 2) << 16) | flat)
    dest_ref[...] = by_slot & SLOT_MASK


def _sort(e2d):
    return pl.pallas_call(
        _sort_kernel,
        out_shape=[jax.ShapeDtypeStruct((SORT_ROWS, LANES), I32),
                   jax.ShapeDtypeStruct((SORT_ROWS, LANES), I32),
                   jax.ShapeDtypeStruct((SUBLANES, LANES), I32)],
        compiler_params=pltpu.CompilerParams(vmem_limit_bytes=VMEM_LIMIT),
        name="route_sort",
    )(e2d)


def _sc_gather_rows(table, idx):
    n = idx.shape[0]
    per_worker = n // SC_WORKERS
    n_chunks = per_worker // SC_ROWS
    assert per_worker * SC_WORKERS == n and n_chunks * SC_ROWS == per_worker and n_chunks % 2 == 0
    mesh = plsc.VectorSubcoreMesh(core_axis_name="core", subcore_axis_name="subcore",
                                  num_cores=SC_CORES, num_subcores=SC_SUBCORES)
    row = (SC_ROWS, ROW_CHUNKS, LANES)

    def body(table_hbm, idx_hbm, out_hbm, iv0, iv1, r0, r1, is0, is1, gs0, gs1):
        idx_v, rows, idx_sem, row_sem = (iv0, iv1), (r0, r1), (is0, is1), (gs0, gs1)
        worker = lax.axis_index("subcore") * SC_CORES + lax.axis_index("core")
        base = worker * per_worker

        def idx_copy(i, b):
            return pltpu.make_async_copy(idx_hbm.at[pl.ds(base + i * SC_ROWS, SC_ROWS)],
                                         idx_v[b], idx_sem[b])

        def row_gather(b):
            return pltpu.make_async_copy(table_hbm.at[idx_v[b]], rows[b], row_sem[b])

        idx_copy(0, 0).start()
        idx_copy(0, 0).wait()
        row_gather(0).start()
        idx_copy(1, 1).start()

        @pl.loop(0, n_chunks // 2)
        def _(pair):
            for b in (0, 1):
                i = 2 * pair + b
                row_gather(b).wait()

                @pl.when(i + 1 < n_chunks)
                def _():
                    idx_copy(i + 1, 1 - b).wait()
                    row_gather(1 - b).start()

                @pl.when(i + 2 < n_chunks)
                def _():
                    idx_copy(i + 2, b).start()

                pltpu.sync_copy(rows[b], out_hbm.at[pl.ds(base + i * SC_ROWS, SC_ROWS)])

    return pl.kernel(
        body, mesh=mesh,
        out_type=jax.ShapeDtypeStruct((n, ROW_CHUNKS, LANES), F32),
        scratch_types=[pltpu.VMEM((SC_ROWS,), I32), pltpu.VMEM((SC_ROWS,), I32),
                       pltpu.VMEM(row, F32), pltpu.VMEM(row, F32),
                       pltpu.SemaphoreType.DMA, pltpu.SemaphoreType.DMA,
                       pltpu.SemaphoreType.DMA, pltpu.SemaphoreType.DMA],
    )(table, idx)


def _expert_kernel(bexp_ref, bp0_ref, bend_ref, bfirst_ref, nused_ref,
                   xs_hbm, wgu_ref, bgu_ref, wd_ref, bd_ref, yb_hbm,
                   xbuf, obuf, wgu_bf, wd_bf, in_sem, out_sem):
    j = pl.program_id(0)
    n_used = nused_ref[0]

    def n_rows(jj):
        return jnp.clip(bend_ref[jj] - bp0_ref[jj], 0, BM)

    def slab_in(jj, buf):
        n = pl.multiple_of(n_rows(jj) * ROW_CHUNKS, ROW_CHUNKS)
        p0 = pl.multiple_of(bp0_ref[jj] * ROW_CHUNKS, ROW_CHUNKS)
        return pltpu.make_async_copy(xs_hbm.at[pl.ds(p0, n)],
                                     xbuf.at[buf, pl.ds(0, n)], in_sem.at[buf])

    def slab_out(jj, buf):
        n = pl.multiple_of(n_rows(jj) * ROW_CHUNKS, ROW_CHUNKS)
        p0 = pl.multiple_of(bp0_ref[jj] * ROW_CHUNKS, ROW_CHUNKS)
        return pltpu.make_async_copy(obuf.at[buf, pl.ds(0, n)],
                                     yb_hbm.at[pl.ds(p0, n)], out_sem.at[buf])

    @pl.when(j == 0)
    def _():
        xbuf[...] = jnp.zeros_like(xbuf)
        slab_in(0, 0).start()

    @pl.when(j < n_used)
    def _():
        buf = j % 2
        slab_in(j, buf).wait()

        @pl.when(j + 1 < n_used)
        def _():
            slab_in(j + 1, 1 - buf).start()

        @pl.when(bfirst_ref[j] == 1)
        def _():
            wgu_bf[...] = wgu_ref[0].astype(BF16)
            wd_bf[...] = wd_ref[0].astype(BF16)

        x = jnp.concatenate([xbuf[buf, pl.ds(cc, BM, stride=ROW_CHUNKS), :]
                             for cc in range(ROW_CHUNKS)], axis=1).astype(BF16)
        gu = jnp.dot(x, wgu_bf[...], preferred_element_type=F32) + bgu_ref[0]
        glu = jnp.minimum(gu[:, :D], SWIGLU_LIMIT)
        lin = jnp.clip(gu[:, D:], -SWIGLU_LIMIT, SWIGLU_LIMIT)
        act = glu * _sigmoid(SWIGLU_ALPHA * glu) * (lin + 1.0)
        y = jnp.dot(act.astype(BF16), wd_bf[...], preferred_element_type=F32) + bd_ref[0]

        @pl.when(j >= 2)
        def _():
            slab_out(j - 2, buf).wait()

        for cc in range(ROW_CHUNKS):
            obuf[buf, pl.ds(cc, BM, stride=ROW_CHUNKS), :] = y[:, cc * LANES:(cc + 1) * LANES]
        slab_out(j, buf).start()

        @pl.when(j == n_used - 1)
        def _():
            slab_out(j, buf).wait()

            @pl.when(j >= 1)
            def _():
                slab_out(j - 1, 1 - buf).wait()


def _experts(bexp, bp0, bend, bfirst, nused, xs, w_gu, b_gu, w_down, b_down):
    grid_spec = pltpu.PrefetchScalarGridSpec(
        num_scalar_prefetch=5,
        grid=(N_BLOCKS,),
        in_specs=[pl.BlockSpec(memory_space=pl.ANY),
                  pl.BlockSpec((1, D, 2 * D), lambda j, be, *_: (be[j], 0, 0)),
                  pl.BlockSpec((1, 1, 2 * D), lambda j, be, *_: (be[j], 0, 0)),
                  pl.BlockSpec((1, D, D), lambda j, be, *_: (be[j], 0, 0)),
                  pl.BlockSpec((1, 1, D), lambda j, be, *_: (be[j], 0, 0))],
        out_specs=pl.BlockSpec(memory_space=pl.ANY),
        scratch_shapes=[pltpu.VMEM((2, BM * ROW_CHUNKS, LANES), F32),
                        pltpu.VMEM((2, BM * ROW_CHUNKS, LANES), F32),
                        pltpu.VMEM((D, 2 * D), BF16),
                        pltpu.VMEM((D, D), BF16),
                        pltpu.SemaphoreType.DMA((2,)),
                        pltpu.SemaphoreType.DMA((2,))],
    )
    return pl.pallas_call(
        _expert_kernel,
        grid_spec=grid_spec,
        out_shape=jax.ShapeDtypeStruct((N_SLOTS * ROW_CHUNKS, LANES), F32),
        compiler_params=pltpu.CompilerParams(
            dimension_semantics=("arbitrary",), vmem_limit_bytes=VMEM_LIMIT),
        name="experts",
    )(bexp, bp0, bend, bfirst, nused, xs, w_gu, b_gu, w_down, b_down)


def _final_kernel(y0_ref, y1_ref, y2_ref, y3_ref, grow_ref, x1_ref, mod_ref, nw_ref, *rest):
    o_ref = rest[-1]
    gr = grow_ref[0]
    gates = [gr[:, k:k + 1] for k in range(TOP_K)]
    y = jnp.concatenate(
        [sum(gates[k] * ref[pl.ds(cc, TM_FINAL, stride=ROW_CHUNKS), :]
             for k, ref in enumerate((y0_ref, y1_ref, y2_ref, y3_ref)))
         for cc in range(ROW_CHUNKS)], axis=1)
    x2 = x1_ref[0] + mod_ref[0, 5:6, :] * y
    ms = jnp.mean(x2 * x2, axis=-1, keepdims=True)
    o_ref[0] = x2 * lax.rsqrt(ms + EPS) * nw_ref[...]


def _final(b0, yg, grow, x1, mod, nw, out_so_far=None):
    tm = TM_FINAL
    per_b = S // tm

    def yspec(k):
        return pl.BlockSpec((tm * ROW_CHUNKS, LANES),
                            lambda b, i: ((b * TOP_K + k) * per_b + i, 0))

    tok = lambda n: pl.BlockSpec((1, tm, n), lambda b, i: (b, i, 0))
    in_specs = [yspec(0), yspec(1), yspec(2), yspec(3), tok(LANES), tok(D),
                pl.BlockSpec((1, 6, D), lambda b, i: (b + b0, 0, 0)),
                _resident((1, D))]
    args = [yg, yg, yg, yg, grow, x1, mod, nw]
    aliases = {}
    if out_so_far is not None:
        in_specs.append(pl.BlockSpec(memory_space=pl.ANY))
        aliases = {len(args): 0}
        args.append(out_so_far)
    return pl.pallas_call(
        _final_kernel,
        grid=(GB, per_b),
        in_specs=in_specs,
        out_specs=pl.BlockSpec((1, tm, D), lambda b, i: (b + b0, i, 0)),
        out_shape=jax.ShapeDtypeStruct((NB, S, D), F32),
        input_output_aliases=aliases,
        compiler_params=pltpu.CompilerParams(
            dimension_semantics=("arbitrary", "arbitrary"), vmem_limit_bytes=VMEM_LIMIT),
        name="final",
    )(*args)


def _block_tables(counts):
    nblk = (counts + BM - 1) // BM
    blk_cum = jnp.cumsum(nblk)
    n_used = blk_cum[-1]
    blk_start = blk_cum - nblk
    start = jnp.cumsum(counts) - counts
    jj = jnp.arange(N_BLOCKS, dtype=I32)
    jc = jnp.minimum(jj, n_used - 1)[:, None]
    owner = ((blk_start[None, :] <= jc) & (jc < blk_cum[None, :])).astype(I32)
    pick = lambda v: jnp.sum(owner * v[None, :], axis=1)
    e_j = pick(jnp.arange(N_EXP, dtype=I32))
    bp0 = pick(start) + (jj - pick(blk_start)) * BM
    bend = pick(start + counts)
    bfirst = (jj == pick(blk_start)).astype(I32)
    return e_j, bp0.astype(I32), bend.astype(I32), bfirst, n_used.reshape(1).astype(I32)


def kernel(x, c, w_ada, b_ada, norm_mix_w, w_in, conv_w, conv_b, dt_bias, a_log, d_skip,
           ssd_norm_w, w_ssd_out, w_pool, pool_scale, w_pool_out, w_out, norm_ffn_w,
           w_router, b_router, w_gu, b_gu, w_down, b_down, norm_final_w):
    assert x.shape == (NB, S, D) and w_ada.shape[0] == 1

    mod = _adaln(c, w_ada[0], b_ada).reshape(NB, 6, D)

    w_in_b = w_in[0].astype(BF16)
    o1, o2, o3, o4 = D_SSD, D_SSD + D_XBC, D_SSD + D_XBC + N_HEADS, D_SSD + D_XBC + N_HEADS + D
    wz, wx, wp, wg = w_in_b[:, :o1], w_in_b[:, o1:o2], w_in_b[:, o3:o4], w_in_b[:, o4:]
    wdt = jnp.pad(w_in_b[:, o2:o3], ((0, 0), (0, LANES - N_HEADS)))
    z, xbc, dtr, up, gts = _inproj(x, mod, norm_mix_w, wz, wx, wdt, wp, wg, conv_w[0], conv_b)

    pad_h = lambda v: jnp.pad(v[0], (0, LANES - N_HEADS))
    dtb, alog = pad_h(dt_bias), pad_h(a_log)
    head_of_chan = jnp.arange(D_SSD, dtype=I32) // HEAD_DIM
    expand = (jnp.arange(LANES, dtype=I32)[:, None] == head_of_chan[None, :]).astype(BF16)
    dskip = jnp.repeat(d_skip[0], HEAD_DIM).reshape(1, D_SSD)
    y = _ssd(xbc, z, dtr, dtb.reshape(1, LANES), alog.reshape(1, LANES),
             dtb.reshape(LANES, 1), alog.reshape(LANES, 1), dskip, ssd_norm_w, expand)

    merge_w = (w_ssd_out[0].astype(BF16), w_pool[0].astype(BF16), pool_scale,
               w_pool_out[0].astype(BF16), w_out[0].astype(BF16), norm_ffn_w,
               w_router[0].T, b_router.reshape(N_EXP, 1))
    expert_w = (w_gu[0], b_gu[0].reshape(N_EXP, 1, 2 * D), w_down[0], b_down[0].reshape(N_EXP, 1, D))
    rows3 = lambda v: v.reshape(-1, ROW_CHUNKS, LANES)
    lines2 = lambda v: v.reshape(-1, LANES)

    groups = []
    for b0 in range(0, NB, GB):
        x1, h2, eidx, grow = _merge(b0, y, up, gts, x, mod, *merge_w)
        tok_of_row, row_of_slot, counts = _sort(eidx.reshape(SORT_ROWS, LANES))
        tables = _block_tables(counts[0, :N_EXP])
        xs = _sc_gather_rows(rows3(h2), tok_of_row.reshape(N_SLOTS))
        groups.append((b0, x1, grow, tables, xs, row_of_slot))

    combined = []
    for b0, x1, grow, tables, xs, row_of_slot in groups:
        yb = _experts(*tables, lines2(xs), *expert_w)
        yg = _sc_gather_rows(rows3(yb), row_of_slot.reshape(N_SLOTS))
        combined.append((b0, x1, grow, yg))

    out = None
    for b0, x1, grow, yg in combined:
        out = _final(b0, lines2(yg), grow, x1, mod, norm_final_w.reshape(1, D), out)
    return out
```

```python
import functools

import jax
import jax.numpy as jnp
from jax import lax
from jax.experimental import pallas as pl
from jax.experimental.pallas import tpu as pltpu
from jax.experimental.pallas import tpu_sc as plsc

F32 = jnp.float32
BF16 = jnp.bfloat16
I32 = jnp.int32
HIGHEST = lax.Precision.HIGHEST

D = 1024
NB = 4
S = 4096
EPS = 1e-5
D_SSD = 2048
N_HEADS = 32
HEAD_DIM = 64
N_GROUPS = 4
D_STATE = 128
GROUP_W = D_SSD // N_GROUPS
D_XBC = D_SSD + 2 * N_GROUPS * D_STATE
CONV_K = 4
POOL_WINDOWS = (2, 4, 8, 16)
POOL_GW = 256
N_EXP = 32
TOP_K = 4
SWIGLU_LIMIT = 7.0
SWIGLU_ALPHA = 1.702

LANES = 128
SUBLANES = 8

CHUNK = 128
TM_PROJ = 512
TM_MERGE = 512
TM_FINAL = 512
BM = 256
GB = 4
N_SLOTS = GB * S * TOP_K
N_BLOCKS = N_SLOTS // BM + N_EXP
SLOT_MASK = (1 << 16) - 1
HALO = 16
SORT_ROWS = N_SLOTS // LANES
ROW_LINES = D // (2 * LANES)

SC_CORES = 2
SC_SUBCORES = 16
SC_WORKERS = SC_CORES * SC_SUBCORES
SC_ROWS = 64

VMEM_LIMIT = 56 * 1024 * 1024


def _sigmoid(v):
    return 0.5 * jnp.tanh(0.5 * v) + 0.5


def _split3(v):
    hi = v.astype(BF16)
    rest = v - hi.astype(F32)
    mid = rest.astype(BF16)
    lo = (rest - mid.astype(F32)).astype(BF16)
    return hi, mid, lo


HI_HALF = -(1 << 16)


def _store_packed_rows(lines, v):
    rows = v.shape[0]
    bits = lambda u: lax.bitcast_convert_type(u.astype(BF16).astype(F32), I32)
    for cc in range(ROW_LINES):
        lo = bits(v[:, cc * LANES:(cc + 1) * LANES])
        hi = bits(v[:, D // 2 + cc * LANES:D // 2 + (cc + 1) * LANES])
        lines[pl.ds(cc, rows, stride=ROW_LINES), :] = lax.shift_right_logical(lo, 16) | (hi & HI_HALF)


def _load_packed_rows(lines):
    rows = lines.shape[0] // ROW_LINES
    words = [lines[pl.ds(cc, rows, stride=ROW_LINES), :] for cc in range(ROW_LINES)]
    lo = [lax.bitcast_convert_type(w << 16, F32) for w in words]
    hi = [lax.bitcast_convert_type(w & HI_HALF, F32) for w in words]
    return jnp.concatenate(lo + hi, axis=1)


def _softplus(v):
    return jnp.maximum(v, 0.0) + jnp.log(1.0 + jnp.exp(-jnp.abs(v)))


def _resident(shape):
    nd = len(shape)
    return pl.BlockSpec(shape, lambda *_: (0,) * nd, pipeline_mode=pl.Buffered(1))


def _adaln_kernel(c_ref, w_ref, b_ref, o_ref):
    c = c_ref[...]
    act = c * _sigmoid(c)
    o_ref[...] = jnp.dot(act, w_ref[...], precision=HIGHEST,
                         preferred_element_type=F32) + b_ref[...]


def _adaln(c, w_ada, b_ada):
    tn = 1536
    return pl.pallas_call(
        _adaln_kernel,
        grid=(6 * D // tn,),
        in_specs=[pl.BlockSpec((NB, D), lambda j: (0, 0)),
                  pl.BlockSpec((D, tn), lambda j: (0, j)),
                  pl.BlockSpec((1, tn), lambda j: (0, j))],
        out_specs=pl.BlockSpec((NB, tn), lambda j: (0, j)),
        out_shape=jax.ShapeDtypeStruct((NB, 6 * D), F32),
        compiler_params=pltpu.CompilerParams(vmem_limit_bytes=VMEM_LIMIT),
        name="adaln",
    )(c, w_ada, b_ada)


def _inproj_kernel(x_ref, mod_ref, nw_ref, wz_ref, wx_ref, wdt_ref, wp_ref, wg_ref, cw_ref, cb_ref,
                   z_ref, xbc_ref, dt_ref, up_ref, g_ref, ubuf):
    tm = TM_PROJ

    @pl.when(pl.program_id(1) == 0)
    def _():
        ubuf[0:SUBLANES, :] = jnp.zeros((SUBLANES, D_XBC), F32)

    x = x_ref[0]
    ms = jnp.mean(x * x, axis=-1, keepdims=True)
    h = x * lax.rsqrt(ms + EPS) * nw_ref[...]
    h = h * (1.0 + mod_ref[0, 1:2, :]) + mod_ref[0, 0:1, :]
    hb = h.astype(BF16)
    z_ref[0] = jnp.dot(hb, wz_ref[...], preferred_element_type=F32).astype(BF16)

    ubuf[SUBLANES:SUBLANES + tm, :] = jnp.dot(hb, wx_ref[...], preferred_element_type=F32)
    conv = cb_ref[...] + cw_ref[0:1, :] * ubuf[pl.ds(SUBLANES - 3, tm), :]
    for k in range(1, CONV_K):
        conv = conv + cw_ref[k:k + 1, :] * ubuf[pl.ds(SUBLANES - 3 + k, tm), :]
    ubuf[0:SUBLANES, :] = ubuf[tm:tm + SUBLANES, :]
    xbc_ref[0] = (conv * _sigmoid(conv)).astype(BF16)

    dt_ref[0] = jnp.dot(hb, wdt_ref[...], preferred_element_type=F32)
    up_ref[0] = jnp.dot(hb, wp_ref[...], preferred_element_type=F32).astype(BF16)
    g_ref[0] = jnp.dot(hb, wg_ref[...], preferred_element_type=F32).astype(BF16)


def _inproj(x, mod, norm_w, wz, wx, wdt, wp, wg, conv_w, conv_b):
    tm = TM_PROJ
    tok = lambda n: pl.BlockSpec((1, tm, n), lambda b, i: (b, i, 0))
    return pl.pallas_call(
        _inproj_kernel,
        grid=(NB, S // tm),
        in_specs=[tok(D),
                  pl.BlockSpec((1, 6, D), lambda b, i: (b, 0, 0)),
                  _resident((1, D)),
                  _resident((D, D_SSD)), _resident((D, D_XBC)), _resident((D, LANES)),
                  _resident((D, D)), _resident((D, 2 * D)),
                  _resident((CONV_K, D_XBC)), _resident((1, D_XBC))],
        out_specs=[tok(D_SSD), tok(D_XBC), tok(LANES), tok(D), tok(2 * D)],
        out_shape=[jax.ShapeDtypeStruct((NB, S, D_SSD), BF16),
                   jax.ShapeDtypeStruct((NB, S, D_XBC), BF16),
                   jax.ShapeDtypeStruct((NB, S, LANES), F32),
                   jax.ShapeDtypeStruct((NB, S, D), BF16),
                   jax.ShapeDtypeStruct((NB, S, 2 * D), BF16)],
        scratch_shapes=[pltpu.VMEM((SUBLANES + tm, D_XBC), F32)],
        compiler_params=pltpu.CompilerParams(
            dimension_semantics=("arbitrary", "arbitrary"), vmem_limit_bytes=VMEM_LIMIT),
        name="inproj",
    )(x, mod, norm_w, wz, wx, wdt, wp, wg, conv_w, conv_b)


def _ssd_kernel(xbc_ref, z_ref, dtr_ref, dtb_row_ref, alog_row_ref,
                dtb_col_ref, alog_col_ref, dskip_ref, nw_ref, expand_ref,
                y_ref, state, ybuf):
    c = pl.program_id(1)

    @pl.when(c == 0)
    def _():
        state[...] = jnp.zeros_like(state)

    xc = xbc_ref[0]

    row = lax.broadcasted_iota(I32, (CHUNK, CHUNK), 0)
    col = lax.broadcasted_iota(I32, (CHUNK, CHUNK), 1)
    causal = row >= col
    tril = causal.astype(BF16)
    triu = (row <= col).astype(BF16)

    def ones_dot_cols(ones, v):
        parts = jnp.dot(ones, jnp.concatenate(_split3(v), axis=1), preferred_element_type=F32)
        n = v.shape[1]
        return parts[:, :n] + parts[:, n:2 * n] + parts[:, 2 * n:]

    def rows_dot_ones(v, ones):
        parts = jnp.dot(jnp.concatenate(_split3(v), axis=0), ones, preferred_element_type=F32)
        m = v.shape[0]
        return parts[:m] + parts[m:2 * m] + parts[2 * m:]

    dtr = dtr_ref[0]
    dt = _softplus(dtr + dtb_row_ref[...])
    a_cum = ones_dot_cols(tril, dt * (-jnp.exp(alog_row_ref[...])))
    dt_t = _softplus(dtr.T + dtb_col_ref[...])
    a_cum_t = rows_dot_ones(dt_t * (-jnp.exp(alog_col_ref[...])), triu)
    w_state_t = dt_t * jnp.exp(a_cum_t[:, CHUNK - 1:CHUNK] - a_cum_t)
    exp_a = jnp.exp(a_cum)
    chunk_decay = rows_dot_ones(exp_a[CHUNK - SUBLANES:CHUNK, :],
                                expand_ref[...])[SUBLANES - 1:SUBLANES, :]

    lane = lax.broadcasted_iota(I32, (CHUNK, LANES), 1)
    first_half = lane < HEAD_DIM

    def block_diag(v):
        v = v.astype(BF16)
        zero = jnp.zeros_like(v)
        return jnp.concatenate([jnp.where(first_half, v, zero),
                                jnp.where(first_half, zero, v)], axis=0)

    for g in range(N_GROUPS):
        b_g = xc[:, D_SSD + g * D_STATE:D_SSD + (g + 1) * D_STATE]
        c_g = xc[:, D_SSD + (N_GROUPS + g) * D_STATE:D_SSD + (N_GROUPS + g + 1) * D_STATE]
        cb = lax.dot_general(c_g, b_g, (((1,), (1,)), ((), ())),
                             preferred_element_type=F32)
        b_t = b_g.astype(F32).T
        for j in range(GROUP_W // LANES):
            lo = g * GROUP_W + j * LANES
            xp = xc[:, lo:lo + LANES]
            rhs_x = block_diag(xp)
            prev = state[g, :, j * LANES:(j + 1) * LANES]
            rhs_p = block_diag(prev)
            m_parts, o_parts, s_parts = [], [], []
            for hh in range(2):
                h = (lo // HEAD_DIM) + hh
                seg = a_cum[:, h:h + 1] - a_cum_t[h:h + 1, :]
                decay = jnp.exp(jnp.where(causal, seg, -jnp.inf))
                m_parts.append(cb * decay * dt_t[h:h + 1, :])
                o_parts.append(c_g * exp_a[:, h:h + 1])
                s_parts.append(b_t * w_state_t[h:h + 1, :])
            lhs_y = jnp.concatenate(m_parts, axis=1).astype(BF16)
            lhs_o = jnp.concatenate(o_parts, axis=1).astype(BF16)
            lhs_s = jnp.concatenate(s_parts, axis=1).astype(BF16)
            y_pair = (jnp.dot(lhs_y, rhs_x, preferred_element_type=F32)
                      + jnp.dot(lhs_o, rhs_p, preferred_element_type=F32))
            st_pair = jnp.dot(lhs_s, rhs_x, preferred_element_type=F32)
            state[g, :, j * LANES:(j + 1) * LANES] = (
                prev * chunk_decay[:, lo:lo + LANES] + st_pair)
            ybuf[:, lo:lo + LANES] = y_pair + dskip_ref[:, lo:lo + LANES] * xp

    zf = z_ref[0].astype(F32)
    yg = ybuf[...] * (zf * _sigmoid(zf))
    for g in range(N_GROUPS):
        blk = yg[:, g * GROUP_W:(g + 1) * GROUP_W]
        ms = jnp.mean(blk * blk, axis=-1, keepdims=True)
        y_ref[0, :, g * GROUP_W:(g + 1) * GROUP_W] = (
            blk * lax.rsqrt(ms + EPS) * nw_ref[:, g * GROUP_W:(g + 1) * GROUP_W]).astype(BF16)


def _ssd(xbc, z, dtr, dtb_row, alog_row, dtb_col, alog_col, dskip, norm_w, expand):
    tok = lambda n: pl.BlockSpec((1, CHUNK, n), lambda b, i: (b, i, 0))
    return pl.pallas_call(
        _ssd_kernel,
        grid=(NB, S // CHUNK),
        in_specs=[tok(D_XBC), tok(D_SSD), tok(LANES),
                  _resident((1, LANES)), _resident((1, LANES)),
                  _resident((LANES, 1)), _resident((LANES, 1)),
                  _resident((1, D_SSD)), _resident((1, D_SSD)),
                  _resident((LANES, D_SSD))],
        out_specs=tok(D_SSD),
        out_shape=jax.ShapeDtypeStruct((NB, S, D_SSD), BF16),
        scratch_shapes=[pltpu.VMEM((N_GROUPS, D_STATE, GROUP_W), F32),
                        pltpu.VMEM((CHUNK, D_SSD), F32)],
        compiler_params=pltpu.CompilerParams(
            dimension_semantics=("arbitrary", "arbitrary"), vmem_limit_bytes=VMEM_LIMIT),
        name="ssd",
    )(xbc, z, dtr, dtb_row, alog_row, dtb_col, alog_col, dskip, norm_w, expand)


def _merge_kernel(y_ref, up_ref, g_ref, x_ref, mod_ref, wso_ref, wpool_ref, pscale_ref,
                  wpo_ref, wout_ref, nw_ref, wr_ref, br_ref,
                  x1_ref, h2_ref, eidx_ref, grow_ref, ubuf):
    i = pl.program_id(1)
    tm = TM_MERGE

    @pl.when(i == 0)
    def _():
        ubuf[0:HALO, :] = jnp.zeros((HALO, D), F32)

    u = up_ref[0].astype(F32)
    ubuf[HALO:HALO + tm, :] = u
    ext = ubuf[...]
    ubuf[0:HALO, :] = ubuf[tm:tm + HALO, :]

    t_pos = i * tm + lax.broadcasted_iota(I32, (tm, 1), 0) + 1
    pooled = []
    for gi, w in enumerate(POOL_WINDOWS):
        acc = ext[:, gi * POOL_GW:(gi + 1) * POOL_GW]
        span = 1
        while span < w:
            acc = acc + pltpu.roll(acc, span, 0)
            span *= 2
        cnt = jnp.minimum(t_pos, w).astype(F32)
        p = acc[HALO:, :] / cnt - u[:, gi * POOL_GW:(gi + 1) * POOL_GW]
        pooled.append(jnp.dot(p.astype(BF16), wpool_ref[gi], preferred_element_type=F32))
    p = jnp.concatenate(pooled, axis=1) * pscale_ref[...]
    y_pool = jnp.dot(p.astype(BF16), wpo_ref[...], preferred_element_type=F32)
    y_ssd = jnp.dot(y_ref[0], wso_ref[...], preferred_element_type=F32)

    gt = _sigmoid(g_ref[0].astype(F32))
    merged = gt[:, :D] * y_ssd + gt[:, D:] * y_pool
    mix = jnp.dot(merged.astype(BF16), wout_ref[...], preferred_element_type=F32)
    x1 = x_ref[0] + mod_ref[0, 2:3, :] * mix
    x1_ref[0] = x1

    ms = jnp.mean(x1 * x1, axis=-1, keepdims=True)
    h2 = x1 * lax.rsqrt(ms + EPS) * nw_ref[...]
    h2 = h2 * (1.0 + mod_ref[0, 4:5, :]) + mod_ref[0, 3:4, :]
    _store_packed_rows(h2_ref.at[0], h2)

    logits = lax.dot_general(wr_ref[...], h2, (((1,), (1,)), ((), ())), precision=HIGHEST,
                             preferred_element_type=F32) + br_ref[...]
    e_iota = lax.broadcasted_iota(I32, (N_EXP, tm), 0)
    vals, idxs = [], []
    for _ in range(TOP_K):
        m = jnp.max(logits, axis=0, keepdims=True)
        idx = jnp.min(jnp.where(logits == m, e_iota, N_EXP), axis=0, keepdims=True)
        vals.append(m)
        idxs.append(idx)
        logits = jnp.where(e_iota == idx, -jnp.inf, logits)
    exps = [jnp.exp(v - vals[0]) for v in vals]
    denom = exps[0] + exps[1] + exps[2] + exps[3]
    eidx_ref[0] = jnp.concatenate(idxs, axis=0)
    gates_t = jnp.concatenate([e / denom for e in exps]
                              + [jnp.zeros((LANES - TOP_K, tm), F32)], axis=0)
    grow_ref[0] = gates_t.T


def _merge(b0, y, up, gts, x, mod, wso, wpool, pscale, wpo, wout, nw, wr_t, br_col):
    tm = TM_MERGE
    tok_in = lambda n: pl.BlockSpec((1, tm, n), lambda b, i: (b + b0, i, 0))
    tok = lambda n: pl.BlockSpec((1, tm, n), lambda b, i: (b, i, 0))
    return pl.pallas_call(
        _merge_kernel,
        grid=(GB, S // tm),
        in_specs=[tok_in(D_SSD), tok_in(D), tok_in(2 * D), tok_in(D),
                  pl.BlockSpec((1, 6, D), lambda b, i: (b + b0, 0, 0)),
                  _resident((D_SSD, D)), _resident((len(POOL_WINDOWS), POOL_GW, POOL_GW)),
                  _resident((1, D)), _resident((D, D)), _resident((D, D)),
                  _resident((1, D)), _resident((N_EXP, D)), _resident((N_EXP, 1))],
        out_specs=[tok(D),
                   pl.BlockSpec((1, tm * ROW_LINES, LANES), lambda b, i: (b, i, 0)),
                   pl.BlockSpec((1, TOP_K, tm), lambda b, i: (b, 0, i)),
                   tok(LANES)],
        out_shape=[jax.ShapeDtypeStruct((GB, S, D), F32),
                   jax.ShapeDtypeStruct((GB, S * ROW_LINES, LANES), I32),
                   jax.ShapeDtypeStruct((GB, TOP_K, S), I32),
                   jax.ShapeDtypeStruct((GB, S, LANES), F32)],
        scratch_shapes=[pltpu.VMEM((HALO + tm, D), F32)],
        compiler_params=pltpu.CompilerParams(
            dimension_semantics=("arbitrary", "arbitrary"), vmem_limit_bytes=VMEM_LIMIT),
        name="merge",
    )(y, up, gts, x, mod, wso, wpool, pscale, wpo, wout, nw, wr_t, br_col)


def _sort_kernel(e_ref, tok_ref, dest_ref, counts_ref):
    row = lax.broadcasted_iota(I32, (SORT_ROWS, LANES), 0)
    lane = lax.broadcasted_iota(I32, (SORT_ROWS, LANES), 1)
    flat = row * LANES + lane
    s_bits = S.bit_length() - 1
    k_bits = TOP_K.bit_length() - 1
    e = e_ref[...]
    keys = (e << 16) | flat

    lane_e = lax.broadcasted_iota(I32, (SUBLANES, LANES), 1)
    counts = jnp.zeros((SUBLANES, LANES), I32)
    for ex in range(N_EXP):
        n = jnp.sum((e == ex).astype(F32))
        counts = jnp.where(lane_e == ex, n.astype(I32), counts)
    counts_ref[...] = counts

    def partner(v, d):
        if d < LANES:
            down = pltpu.roll(v, d, 1)
            up = pltpu.roll(v, LANES - d, 1)
            return jnp.where((lane & d) != 0, down, up)
        r = d // LANES
        down = pltpu.roll(v, r, 0)
        up = pltpu.roll(v, SORT_ROWS - r, 0)
        return jnp.where((row & r) != 0, down, up)

    def bitonic_sort(keys):
        k = 2
        while k <= N_SLOTS:
            d = k // 2
            while d >= 1:
                other = partner(keys, d)
                is_low = (flat & d) == 0
                ascending = (flat & k) == 0
                take_min = is_low == ascending
                keys = jnp.where(take_min, jnp.minimum(keys, other), jnp.maximum(keys, other))
                d //= 2
            k *= 2
        return keys

    by_expert = bitonic_sort(keys) & SLOT_MASK
    tok_ref[...] = ((by_expert >> (s_bits + k_bits)) << s_bits) | (by_expert & (S - 1))
    by_slot = bitonic_sort(((by_expert - N_SLOTS // 2) << 16) | flat)
    dest_ref[...] = by_slot & SLOT_MASK


def _sort(e2d):
    return pl.pallas_call(
        _sort_kernel,
        out_shape=[jax.ShapeDtypeStruct((SORT_ROWS, LANES), I32),
                   jax.ShapeDtypeStruct((SORT_ROWS, LANES), I32),
                   jax.ShapeDtypeStruct((SUBLANES, LANES), I32)],
        compiler_params=pltpu.CompilerParams(vmem_limit_bytes=VMEM_LIMIT),
        name="route_sort",
    )(e2d)


def _sc_gather_rows(table, idx):
    n = idx.shape[0]
    per_worker = n // SC_WORKERS
    n_chunks = per_worker // SC_ROWS
    assert per_worker * SC_WORKERS == n and n_chunks * SC_ROWS == per_worker and n_chunks % 2 == 0
    mesh = plsc.VectorSubcoreMesh(core_axis_name="core", subcore_axis_name="subcore",
                                  num_cores=SC_CORES, num_subcores=SC_SUBCORES)
    row = (SC_ROWS, ROW_LINES, LANES)

    def body(table_hbm, idx_hbm, out_hbm, iv0, iv1, r0, r1, is0, is1, gs0, gs1):
        idx_v, rows, idx_sem, row_sem = (iv0, iv1), (r0, r1), (is0, is1), (gs0, gs1)
        worker = lax.axis_index("subcore") * SC_CORES + lax.axis_index("core")
        base = worker * per_worker

        def idx_copy(i, b):
            return pltpu.make_async_copy(idx_hbm.at[pl.ds(base + i * SC_ROWS, SC_ROWS)],
                                         idx_v[b], idx_sem[b])

        def row_gather(b):
            return pltpu.make_async_copy(table_hbm.at[idx_v[b]], rows[b], row_sem[b])

        idx_copy(0, 0).start()
        idx_copy(0, 0).wait()
        row_gather(0).start()
        idx_copy(1, 1).start()

        @pl.loop(0, n_chunks // 2)
        def _(pair):
            for b in (0, 1):
                i = 2 * pair + b
                row_gather(b).wait()

                @pl.when(i + 1 < n_chunks)
                def _():
                    idx_copy(i + 1, 1 - b).wait()
                    row_gather(1 - b).start()

                @pl.when(i + 2 < n_chunks)
                def _():
                    idx_copy(i + 2, b).start()

                pltpu.sync_copy(rows[b], out_hbm.at[pl.ds(base + i * SC_ROWS, SC_ROWS)])

    return pl.kernel(
        body, mesh=mesh,
        out_type=jax.ShapeDtypeStruct((n, ROW_LINES, LANES), I32),
        scratch_types=[pltpu.VMEM((SC_ROWS,), I32), pltpu.VMEM((SC_ROWS,), I32),
                       pltpu.VMEM(row, I32), pltpu.VMEM(row, I32),
                       pltpu.SemaphoreType.DMA, pltpu.SemaphoreType.DMA,
                       pltpu.SemaphoreType.DMA, pltpu.SemaphoreType.DMA],
    )(table, idx)


def _expert_kernel(bexp_ref, bp0_ref, bend_ref, bfirst_ref, nused_ref,
                   xs_hbm, wgu_ref, bgu_ref, wd_ref, bd_ref, yb_hbm,
                   xbuf, obuf, wgu_bf, wd_bf, in_sem, out_sem):
    j = pl.program_id(0)
    n_used = nused_ref[0]

    def n_rows(jj):
        return jnp.clip(bend_ref[jj] - bp0_ref[jj], 0, BM)

    def slab_in(jj, buf):
        n = n_rows(jj)
        return pltpu.make_async_copy(xs_hbm.at[pl.ds(bp0_ref[jj], n)],
                                     xbuf.at[buf, pl.ds(0, n)], in_sem.at[buf])

    def slab_out(jj, buf):
        n = n_rows(jj)
        return pltpu.make_async_copy(obuf.at[buf, pl.ds(0, n)],
                                     yb_hbm.at[pl.ds(bp0_ref[jj], n)], out_sem.at[buf])

    @pl.when(j == 0)
    def _():
        xbuf[...] = jnp.zeros_like(xbuf)
        slab_in(0, 0).start()

    @pl.when(j < n_used)
    def _():
        buf = j % 2
        slab_in(j, buf).wait()

        @pl.when(j + 1 < n_used)
        def _():
            slab_in(j + 1, 1 - buf).start()

        @pl.when(bfirst_ref[j] == 1)
        def _():
            wgu_bf[...] = wgu_ref[0].astype(BF16)
            wd_bf[...] = wd_ref[0].astype(BF16)

        lines_of = lambda ref: ref.at[buf].reshape(BM * ROW_LINES, LANES)
        x = _load_packed_rows(lines_of(xbuf)).astype(BF16)
        gu = jnp.dot(x, wgu_bf[...], preferred_element_type=F32) + bgu_ref[0]
        glu = jnp.minimum(gu[:, :D], SWIGLU_LIMIT)
        lin = jnp.clip(gu[:, D:], -SWIGLU_LIMIT, SWIGLU_LIMIT)
        act = glu * _sigmoid(SWIGLU_ALPHA * glu) * (lin + 1.0)
        y = jnp.dot(act.astype(BF16), wd_bf[...], preferred_element_type=F32) + bd_ref[0]

        @pl.when(j >= 2)
        def _():
            slab_out(j - 2, buf).wait()

        _store_packed_rows(lines_of(obuf), y)
        slab_out(j, buf).start()

        @pl.when(j == n_used - 1)
        def _():
            slab_out(j, buf).wait()

            @pl.when(j >= 1)
            def _():
                slab_out(j - 1, 1 - buf).wait()


def _experts(bexp, bp0, bend, bfirst, nused, xs, w_gu, b_gu, w_down, b_down):
    grid_spec = pltpu.PrefetchScalarGridSpec(
        num_scalar_prefetch=5,
        grid=(N_BLOCKS,),
        in_specs=[pl.BlockSpec(memory_space=pl.ANY),
                  pl.BlockSpec((1, D, 2 * D), lambda j, be, *_: (be[j], 0, 0)),
                  pl.BlockSpec((1, 1, 2 * D), lambda j, be, *_: (be[j], 0, 0)),
                  pl.BlockSpec((1, D, D), lambda j, be, *_: (be[j], 0, 0)),
                  pl.BlockSpec((1, 1, D), lambda j, be, *_: (be[j], 0, 0))],
        out_specs=pl.BlockSpec(memory_space=pl.ANY),
        scratch_shapes=[pltpu.VMEM((2, BM, ROW_LINES, LANES), I32),
                        pltpu.VMEM((2, BM, ROW_LINES, LANES), I32),
                        pltpu.VMEM((D, 2 * D), BF16),
                        pltpu.VMEM((D, D), BF16),
                        pltpu.SemaphoreType.DMA((2,)),
                        pltpu.SemaphoreType.DMA((2,))],
    )
    return pl.pallas_call(
        _expert_kernel,
        grid_spec=grid_spec,
        out_shape=jax.ShapeDtypeStruct((N_SLOTS, ROW_LINES, LANES), I32),
        compiler_params=pltpu.CompilerParams(
            dimension_semantics=("arbitrary",), vmem_limit_bytes=VMEM_LIMIT),
        name="experts",
    )(bexp, bp0, bend, bfirst, nused, xs, w_gu, b_gu, w_down, b_down)


def _final_kernel(y0_ref, y1_ref, y2_ref, y3_ref, grow_ref, x1_ref, mod_ref, nw_ref, *rest):
    o_ref = rest[-1]
    gr = grow_ref[0]
    gates = [gr[:, k:k + 1] for k in range(TOP_K)]
    y = sum(gates[k] * _load_packed_rows(ref)
            for k, ref in enumerate((y0_ref, y1_ref, y2_ref, y3_ref)))
    x2 = x1_ref[0] + mod_ref[0, 5:6, :] * y
    ms = jnp.mean(x2 * x2, axis=-1, keepdims=True)
    o_ref[0] = x2 * lax.rsqrt(ms + EPS) * nw_ref[...]


def _final(b0, yg, grow, x1, mod, nw, out_so_far=None):
    tm = TM_FINAL
    per_b = S // tm

    def yspec(k):
        return pl.BlockSpec((tm * ROW_LINES, LANES),
                            lambda b, i: ((b * TOP_K + k) * per_b + i, 0))

    tok = lambda n: pl.BlockSpec((1, tm, n), lambda b, i: (b, i, 0))
    in_specs = [yspec(0), yspec(1), yspec(2), yspec(3), tok(LANES), tok(D),
                pl.BlockSpec((1, 6, D), lambda b, i: (b + b0, 0, 0)),
                _resident((1, D))]
    args = [yg, yg, yg, yg, grow, x1, mod, nw]
    aliases = {}
    if out_so_far is not None:
        in_specs.append(pl.BlockSpec(memory_space=pl.ANY))
        aliases = {len(args): 0}
        args.append(out_so_far)
    return pl.pallas_call(
        _final_kernel,
        grid=(GB, per_b),
        in_specs=in_specs,
        out_specs=pl.BlockSpec((1, tm, D), lambda b, i: (b + b0, i, 0)),
        out_shape=jax.ShapeDtypeStruct((NB, S, D), F32),
        input_output_aliases=aliases,
        compiler_params=pltpu.CompilerParams(
            dimension_semantics=("arbitrary", "arbitrary"), vmem_limit_bytes=VMEM_LIMIT),
        name="final",
    )(*args)


def _block_tables(counts):
    nblk = (counts + BM - 1) // BM
    blk_cum = jnp.cumsum(nblk)
    n_used = blk_cum[-1]
    blk_start = blk_cum - nblk
    start = jnp.cumsum(counts) - counts
    jj = jnp.arange(N_BLOCKS, dtype=I32)
    jc = jnp.minimum(jj, n_used - 1)[:, None]
    owner = ((blk_start[None, :] <= jc) & (jc < blk_cum[None, :])).astype(I32)
    pick = lambda v: jnp.sum(owner * v[None, :], axis=1)
    e_j = pick(jnp.arange(N_EXP, dtype=I32))
    bp0 = pick(start) + (jj - pick(blk_start)) * BM
    bend = pick(start + counts)
    bfirst = (jj == pick(blk_start)).astype(I32)
    return e_j, bp0.astype(I32), bend.astype(I32), bfirst, n_used.reshape(1).astype(I32)


def kernel(x, c, w_ada, b_ada, norm_mix_w, w_in, conv_w, conv_b, dt_bias, a_log, d_skip,
           ssd_norm_w, w_ssd_out, w_pool, pool_scale, w_pool_out, w_out, norm_ffn_w,
           w_router, b_router, w_gu, b_gu, w_down, b_down, norm_final_w):
    assert x.shape == (NB, S, D) and w_ada.shape[0] == 1

    mod = _adaln(c, w_ada[0], b_ada).reshape(NB, 6, D)

    w_in_b = w_in[0].astype(BF16)
    o1, o2, o3, o4 = D_SSD, D_SSD + D_XBC, D_SSD + D_XBC + N_HEADS, D_SSD + D_XBC + N_HEADS + D
    wz, wx, wp, wg = w_in_b[:, :o1], w_in_b[:, o1:o2], w_in_b[:, o3:o4], w_in_b[:, o4:]
    wdt = jnp.pad(w_in_b[:, o2:o3], ((0, 0), (0, LANES - N_HEADS)))
    z, xbc, dtr, up, gts = _inproj(x, mod, norm_mix_w, wz, wx, wdt, wp, wg, conv_w[0], conv_b)

    pad_h = lambda v: jnp.pad(v[0], (0, LANES - N_HEADS))
    dtb, alog = pad_h(dt_bias), pad_h(a_log)
    head_of_chan = jnp.arange(D_SSD, dtype=I32) // HEAD_DIM
    expand = (jnp.arange(LANES, dtype=I32)[:, None] == head_of_chan[None, :]).astype(BF16)
    dskip = jnp.repeat(d_skip[0], HEAD_DIM).reshape(1, D_SSD)
    y = _ssd(xbc, z, dtr, dtb.reshape(1, LANES), alog.reshape(1, LANES),
             dtb.reshape(LANES, 1), alog.reshape(LANES, 1), dskip, ssd_norm_w, expand)

    merge_w = (w_ssd_out[0].astype(BF16), w_pool[0].astype(BF16), pool_scale,
               w_pool_out[0].astype(BF16), w_out[0].astype(BF16), norm_ffn_w,
               w_router[0].T, b_router.reshape(N_EXP, 1))
    expert_w = (w_gu[0], b_gu[0].reshape(N_EXP, 1, 2 * D), w_down[0], b_down[0].reshape(N_EXP, 1, D))

    groups = []
    for b0 in range(0, NB, GB):
        x1, h2, eidx, grow = _merge(b0, y, up, gts, x, mod, *merge_w)
        tok_of_row, row_of_slot, counts = _sort(eidx.reshape(SORT_ROWS, LANES))
        tables = _block_tables(counts[0, :N_EXP])
        xs = _sc_gather_rows(h2.reshape(GB * S, ROW_LINES, LANES), tok_of_row.reshape(N_SLOTS))
        groups.append((b0, x1, grow, tables, xs, row_of_slot))

    combined = []
    for b0, x1, grow, tables, xs, row_of_slot in groups:
        yb = _experts(*tables, xs, *expert_w)
        yg = _sc_gather_rows(yb, row_of_slot.reshape(N_SLOTS))
        combined.append((b0, x1, grow, yg))

    out = None
    for b0, x1, grow, yg in combined:
        out = _final(b0, yg.reshape(N_SLOTS * ROW_LINES, LANES), grow, x1, mod,
                     norm_final_w.reshape(1, D), out)
    return out
```

```python
import functools

import jax
import jax.numpy as jnp
from jax import lax
from jax.experimental import pallas as pl
from jax.experimental.pallas import tpu as pltpu
from jax.experimental.pallas import tpu_sc as plsc

F32 = jnp.float32
BF16 = jnp.bfloat16
I32 = jnp.int32
HIGHEST = lax.Precision.HIGHEST

D = 1024
NB = 4
S = 4096
EPS = 1e-5
D_SSD = 2048
N_HEADS = 32
HEAD_DIM = 64
N_GROUPS = 4
D_STATE = 128
GROUP_W = D_SSD // N_GROUPS
D_XBC = D_SSD + 2 * N_GROUPS * D_STATE
CONV_K = 4
POOL_WINDOWS = (2, 4, 8, 16)
POOL_GW = 256
N_EXP = 32
TOP_K = 4
SWIGLU_LIMIT = 7.0
SWIGLU_ALPHA = 1.702

LANES = 128
SUBLANES = 8

CHUNK = 128
TM_PROJ = 512
TM_MERGE = 512
TM_FINAL = 512
BM = 256
GB = 4
N_SLOTS = GB * S * TOP_K
N_BLOCKS = N_SLOTS // BM + N_EXP
SLOT_MASK = (1 << 16) - 1
HALO = 16
SORT_ROWS = N_SLOTS // LANES
ROW_LINES = D // (2 * LANES)

SC_CORES = 2
SC_SUBCORES = 16
SC_WORKERS = SC_CORES * SC_SUBCORES
SC_ROWS = 64

VMEM_LIMIT = 56 * 1024 * 1024


def _sigmoid(v):
    return 0.5 * jnp.tanh(0.5 * v) + 0.5


def _split3(v):
    hi = v.astype(BF16)
    rest = v - hi.astype(F32)
    mid = rest.astype(BF16)
    lo = (rest - mid.astype(F32)).astype(BF16)
    return hi, mid, lo


HI_HALF = -(1 << 16)


def _store_packed_rows(lines, v):
    rows = v.shape[0]
    bits = lambda u: lax.bitcast_convert_type(u.astype(BF16).astype(F32), I32)
    for cc in range(ROW_LINES):
        lo = bits(v[:, cc * LANES:(cc + 1) * LANES])
        hi = bits(v[:, D // 2 + cc * LANES:D // 2 + (cc + 1) * LANES])
        lines[pl.ds(cc, rows, stride=ROW_LINES), :] = lax.shift_right_logical(lo, 16) | (hi & HI_HALF)


def _load_packed_rows(lines):
    rows = lines.shape[0] // ROW_LINES
    words = [lines[pl.ds(cc, rows, stride=ROW_LINES), :] for cc in range(ROW_LINES)]
    lo = [lax.bitcast_convert_type(w << 16, F32) for w in words]
    hi = [lax.bitcast_convert_type(w & HI_HALF, F32) for w in words]
    return jnp.concatenate(lo + hi, axis=1)


def _softplus(v):
    return jnp.maximum(v, 0.0) + jnp.log(1.0 + jnp.exp(-jnp.abs(v)))


def _resident(shape):
    nd = len(shape)
    return pl.BlockSpec(shape, lambda *_: (0,) * nd, pipeline_mode=pl.Buffered(1))


def _adaln_kernel(c_ref, w_ref, b_ref, o_ref):
    c = c_ref[...]
    act = c * _sigmoid(c)
    o_ref[...] = jnp.dot(act, w_ref[...], precision=HIGHEST,
                         preferred_element_type=F32) + b_ref[...]


def _adaln(c, w_ada, b_ada):
    tn = 1536
    return pl.pallas_call(
        _adaln_kernel,
        grid=(6 * D // tn,),
        in_specs=[pl.BlockSpec((NB, D), lambda j: (0, 0)),
                  pl.BlockSpec((D, tn), lambda j: (0, j)),
                  pl.BlockSpec((1, tn), lambda j: (0, j))],
        out_specs=pl.BlockSpec((NB, tn), lambda j: (0, j)),
        out_shape=jax.ShapeDtypeStruct((NB, 6 * D), F32),
        compiler_params=pltpu.CompilerParams(vmem_limit_bytes=VMEM_LIMIT),
        name="adaln",
    )(c, w_ada, b_ada)


def _inproj_kernel(x_ref, mod_ref, nw_ref, wz_ref, wx_ref, wdt_ref, wp_ref, wg_ref, cw_ref, cb_ref,
                   z_ref, xbc_ref, dt_ref, up_ref, g_ref, ubuf):
    tm = TM_PROJ

    @pl.when(pl.program_id(1) == 0)
    def _():
        ubuf[0:SUBLANES, :] = jnp.zeros((SUBLANES, D_XBC), F32)

    x = x_ref[0]
    ms = jnp.mean(x * x, axis=-1, keepdims=True)
    h = x * lax.rsqrt(ms + EPS) * nw_ref[...]
    h = h * (1.0 + mod_ref[0, 1:2, :]) + mod_ref[0, 0:1, :]
    hb = h.astype(BF16)
    z_ref[0] = jnp.dot(hb, wz_ref[...], preferred_element_type=F32).astype(BF16)

    ubuf[SUBLANES:SUBLANES + tm, :] = jnp.dot(hb, wx_ref[...], preferred_element_type=F32)
    conv = cb_ref[...] + cw_ref[0:1, :] * ubuf[pl.ds(SUBLANES - 3, tm), :]
    for k in range(1, CONV_K):
        conv = conv + cw_ref[k:k + 1, :] * ubuf[pl.ds(SUBLANES - 3 + k, tm), :]
    ubuf[0:SUBLANES, :] = ubuf[tm:tm + SUBLANES, :]
    xbc_ref[0] = (conv * _sigmoid(conv)).astype(BF16)

    dt_ref[0] = jnp.dot(hb, wdt_ref[...], preferred_element_type=F32)
    up_ref[0] = jnp.dot(hb, wp_ref[...], preferred_element_type=F32).astype(BF16)
    g_ref[0] = jnp.dot(hb, wg_ref[...], preferred_element_type=F32).astype(BF16)


def _inproj(x, mod, norm_w, wz, wx, wdt, wp, wg, conv_w, conv_b):
    tm = TM_PROJ
    tok = lambda n: pl.BlockSpec((1, tm, n), lambda b, i: (b, i, 0))
    return pl.pallas_call(
        _inproj_kernel,
        grid=(NB, S // tm),
        in_specs=[tok(D),
                  pl.BlockSpec((1, 6, D), lambda b, i: (b, 0, 0)),
                  _resident((1, D)),
                  _resident((D, D_SSD)), _resident((D, D_XBC)), _resident((D, LANES)),
                  _resident((D, D)), _resident((D, 2 * D)),
                  _resident((CONV_K, D_XBC)), _resident((1, D_XBC))],
        out_specs=[tok(D_SSD), tok(D_XBC), tok(LANES), tok(D), tok(2 * D)],
        out_shape=[jax.ShapeDtypeStruct((NB, S, D_SSD), BF16),
                   jax.ShapeDtypeStruct((NB, S, D_XBC), BF16),
                   jax.ShapeDtypeStruct((NB, S, LANES), F32),
                   jax.ShapeDtypeStruct((NB, S, D), BF16),
                   jax.ShapeDtypeStruct((NB, S, 2 * D), BF16)],
        scratch_shapes=[pltpu.VMEM((SUBLANES + tm, D_XBC), F32)],
        compiler_params=pltpu.CompilerParams(
            dimension_semantics=("arbitrary", "arbitrary"), vmem_limit_bytes=VMEM_LIMIT),
        name="inproj",
    )(x, mod, norm_w, wz, wx, wdt, wp, wg, conv_w, conv_b)


def _ssd_kernel(xbc_ref, z_ref, dtr_ref, dtb_row_ref, alog_row_ref,
                dtb_col_ref, alog_col_ref, dskip_ref, nw_ref, expand_ref,
                y_ref, state, ybuf):
    c = pl.program_id(1)

    @pl.when(c == 0)
    def _():
        state[...] = jnp.zeros_like(state)

    xc = xbc_ref[0]

    row = lax.broadcasted_iota(I32, (CHUNK, CHUNK), 0)
    col = lax.broadcasted_iota(I32, (CHUNK, CHUNK), 1)
    causal = row >= col
    tril = causal.astype(BF16)
    triu = (row <= col).astype(BF16)

    def ones_dot_cols(ones, v):
        parts = jnp.dot(ones, jnp.concatenate(_split3(v), axis=1), preferred_element_type=F32)
        n = v.shape[1]
        return parts[:, :n] + parts[:, n:2 * n] + parts[:, 2 * n:]

    def rows_dot_ones(v, ones):
        parts = jnp.dot(jnp.concatenate(_split3(v), axis=0), ones, preferred_element_type=F32)
        m = v.shape[0]
        return parts[:m] + parts[m:2 * m] + parts[2 * m:]

    dtr = dtr_ref[0]
    dt = _softplus(dtr + dtb_row_ref[...])
    a_cum = ones_dot_cols(tril, dt * (-jnp.exp(alog_row_ref[...])))
    dt_t = _softplus(dtr.T + dtb_col_ref[...])
    a_cum_t = rows_dot_ones(dt_t * (-jnp.exp(alog_col_ref[...])), triu)
    w_state_t = dt_t * jnp.exp(a_cum_t[:, CHUNK - 1:CHUNK] - a_cum_t)
    exp_a = jnp.exp(a_cum)
    chunk_decay = rows_dot_ones(exp_a[CHUNK - SUBLANES:CHUNK, :],
                                expand_ref[...])[SUBLANES - 1:SUBLANES, :]

    lane = lax.broadcasted_iota(I32, (CHUNK, LANES), 1)
    first_half = lane < HEAD_DIM

    def block_diag(v):
        v = v.astype(BF16)
        zero = jnp.zeros_like(v)
        return jnp.concatenate([jnp.where(first_half, v, zero),
                                jnp.where(first_half, zero, v)], axis=0)

    for g in range(N_GROUPS):
        b_g = xc[:, D_SSD + g * D_STATE:D_SSD + (g + 1) * D_STATE]
        c_g = xc[:, D_SSD + (N_GROUPS + g) * D_STATE:D_SSD + (N_GROUPS + g + 1) * D_STATE]
        cb = lax.dot_general(c_g, b_g, (((1,), (1,)), ((), ())),
                             preferred_element_type=F32)
        b_t = b_g.astype(F32).T
        for j in range(GROUP_W // LANES):
            lo = g * GROUP_W + j * LANES
            xp = xc[:, lo:lo + LANES]
            rhs_x = block_diag(xp)
            prev = state[g, :, j * LANES:(j + 1) * LANES]
            rhs_p = block_diag(prev)
            m_parts, o_parts, s_parts = [], [], []
            for hh in range(2):
                h = (lo // HEAD_DIM) + hh
                seg = a_cum[:, h:h + 1] - a_cum_t[h:h + 1, :]
                decay = jnp.exp(jnp.where(causal, seg, -jnp.inf))
                m_parts.append(cb * decay * dt_t[h:h + 1, :])
                o_parts.append(c_g * exp_a[:, h:h + 1])
                s_parts.append(b_t * w_state_t[h:h + 1, :])
            lhs_y = jnp.concatenate(m_parts, axis=1).astype(BF16)
            lhs_o = jnp.concatenate(o_parts, axis=1).astype(BF16)
            lhs_s = jnp.concatenate(s_parts, axis=1).astype(BF16)
            y_pair = (jnp.dot(lhs_y, rhs_x, preferred_element_type=F32)
                      + jnp.dot(lhs_o, rhs_p, preferred_element_type=F32))
            st_pair = jnp.dot(lhs_s, rhs_x, preferred_element_type=F32)
            state[g, :, j * LANES:(j + 1) * LANES] = (
                prev * chunk_decay[:, lo:lo + LANES] + st_pair)
            ybuf[:, lo:lo + LANES] = y_pair + dskip_ref[:, lo:lo + LANES] * xp

    zf = z_ref[0].astype(F32)
    yg = ybuf[...] * (zf * _sigmoid(zf))
    for g in range(N_GROUPS):
        blk = yg[:, g * GROUP_W:(g + 1) * GROUP_W]
        ms = jnp.mean(blk * blk, axis=-1, keepdims=True)
        y_ref[0, :, g * GROUP_W:(g + 1) * GROUP_W] = (
            blk * lax.rsqrt(ms + EPS) * nw_ref[:, g * GROUP_W:(g + 1) * GROUP_W]).astype(BF16)


def _ssd(xbc, z, dtr, dtb_row, alog_row, dtb_col, alog_col, dskip, norm_w, expand):
    tok = lambda n: pl.BlockSpec((1, CHUNK, n), lambda b, i: (b, i, 0))
    return pl.pallas_call(
        _ssd_kernel,
        grid=(NB, S // CHUNK),
        in_specs=[tok(D_XBC), tok(D_SSD), tok(LANES),
                  _resident((1, LANES)), _resident((1, LANES)),
                  _resident((LANES, 1)), _resident((LANES, 1)),
                  _resident((1, D_SSD)), _resident((1, D_SSD)),
                  _resident((LANES, D_SSD))],
        out_specs=tok(D_SSD),
        out_shape=jax.ShapeDtypeStruct((NB, S, D_SSD), BF16),
        scratch_shapes=[pltpu.VMEM((N_GROUPS, D_STATE, GROUP_W), F32),
                        pltpu.VMEM((CHUNK, D_SSD), F32)],
        compiler_params=pltpu.CompilerParams(
            dimension_semantics=("arbitrary", "arbitrary"), vmem_limit_bytes=VMEM_LIMIT),
        name="ssd",
    )(xbc, z, dtr, dtb_row, alog_row, dtb_col, alog_col, dskip, norm_w, expand)


def _merge_kernel(y_ref, up_ref, g_ref, x_ref, mod_ref, wso_ref, wpool_ref, pscale_ref,
                  wpo_ref, wout_ref, nw_ref, wr_ref, br_ref,
                  x1_ref, h2_ref, eidx_ref, grow_ref, ubuf):
    i = pl.program_id(1)
    tm = TM_MERGE

    @pl.when(i == 0)
    def _():
        ubuf[0:HALO, :] = jnp.zeros((HALO, D), F32)

    u = up_ref[0].astype(F32)
    ubuf[HALO:HALO + tm, :] = u
    ext = ubuf[...]
    ubuf[0:HALO, :] = ubuf[tm:tm + HALO, :]

    t_pos = i * tm + lax.broadcasted_iota(I32, (tm, 1), 0) + 1
    pooled = []
    for gi, w in enumerate(POOL_WINDOWS):
        acc = ext[:, gi * POOL_GW:(gi + 1) * POOL_GW]
        span = 1
        while span < w:
            acc = acc + pltpu.roll(acc, span, 0)
            span *= 2
        cnt = jnp.minimum(t_pos, w).astype(F32)
        p = acc[HALO:, :] / cnt - u[:, gi * POOL_GW:(gi + 1) * POOL_GW]
        pooled.append(jnp.dot(p.astype(BF16), wpool_ref[gi], preferred_element_type=F32))
    p = jnp.concatenate(pooled, axis=1) * pscale_ref[...]
    y_pool = jnp.dot(p.astype(BF16), wpo_ref[...], preferred_element_type=F32)
    y_ssd = jnp.dot(y_ref[0], wso_ref[...], preferred_element_type=F32)

    gt = _sigmoid(g_ref[0].astype(F32))
    merged = gt[:, :D] * y_ssd + gt[:, D:] * y_pool
    mix = jnp.dot(merged.astype(BF16), wout_ref[...], preferred_element_type=F32)
    x1 = x_ref[0] + mod_ref[0, 2:3, :] * mix
    x1_ref[0] = x1

    ms = jnp.mean(x1 * x1, axis=-1, keepdims=True)
    h2 = x1 * lax.rsqrt(ms + EPS) * nw_ref[...]
    h2 = h2 * (1.0 + mod_ref[0, 4:5, :]) + mod_ref[0, 3:4, :]
    _store_packed_rows(h2_ref.at[0], h2)

    logits = lax.dot_general(wr_ref[...], h2, (((1,), (1,)), ((), ())), precision=HIGHEST,
                             preferred_element_type=F32) + br_ref[...]
    e_iota = lax.broadcasted_iota(I32, (N_EXP, tm), 0)
    vals, idxs = [], []
    for _ in range(TOP_K):
        m = jnp.max(logits, axis=0, keepdims=True)
        idx = jnp.min(jnp.where(logits == m, e_iota, N_EXP), axis=0, keepdims=True)
        vals.append(m)
        idxs.append(idx)
        logits = jnp.where(e_iota == idx, -jnp.inf, logits)
    exps = [jnp.exp(v - vals[0]) for v in vals]
    denom = exps[0] + exps[1] + exps[2] + exps[3]
    eidx_ref[0] = jnp.concatenate(idxs, axis=0)
    gates_t = jnp.concatenate([e / denom for e in exps]
                              + [jnp.zeros((LANES - TOP_K, tm), F32)], axis=0)
    grow_ref[0] = gates_t.T


def _merge(b0, y, up, gts, x, mod, wso, wpool, pscale, wpo, wout, nw, wr_t, br_col):
    tm = TM_MERGE
    tok_in = lambda n: pl.BlockSpec((1, tm, n), lambda b, i: (b + b0, i, 0))
    tok = lambda n: pl.BlockSpec((1, tm, n), lambda b, i: (b, i, 0))
    return pl.pallas_call(
        _merge_kernel,
        grid=(GB, S // tm),
        in_specs=[tok_in(D_SSD), tok_in(D), tok_in(2 * D), tok_in(D),
                  pl.BlockSpec((1, 6, D), lambda b, i: (b + b0, 0, 0)),
                  _resident((D_SSD, D)), _resident((len(POOL_WINDOWS), POOL_GW, POOL_GW)),
                  _resident((1, D)), _resident((D, D)), _resident((D, D)),
                  _resident((1, D)), _resident((N_EXP, D)), _resident((N_EXP, 1))],
        out_specs=[tok(D),
                   pl.BlockSpec((1, tm * ROW_LINES, LANES), lambda b, i: (b, i, 0)),
                   pl.BlockSpec((1, TOP_K, tm), lambda b, i: (b, 0, i)),
                   tok(LANES)],
        out_shape=[jax.ShapeDtypeStruct((GB, S, D), F32),
                   jax.ShapeDtypeStruct((GB, S * ROW_LINES, LANES), I32),
                   jax.ShapeDtypeStruct((GB, TOP_K, S), I32),
                   jax.ShapeDtypeStruct((GB, S, LANES), F32)],
        scratch_shapes=[pltpu.VMEM((HALO + tm, D), F32)],
        compiler_params=pltpu.CompilerParams(
            dimension_semantics=("arbitrary", "arbitrary"), vmem_limit_bytes=VMEM_LIMIT),
        name="merge",
    )(y, up, gts, x, mod, wso, wpool, pscale, wpo, wout, nw, wr_t, br_col)


def _sort_kernel(e_ref, tok_ref, dest_ref, counts_ref):
    row = lax.broadcasted_iota(I32, (SORT_ROWS, LANES), 0)
    lane = lax.broadcasted_iota(I32, (SORT_ROWS, LANES), 1)
    flat = row * LANES + lane
    s_bits = S.bit_length() - 1
    k_bits = TOP_K.bit_length() - 1
    e = e_ref[...]
    keys = (e << 16) | flat

    lane_e = lax.broadcasted_iota(I32, (SUBLANES, LANES), 1)
    counts = jnp.zeros((SUBLANES, LANES), I32)
    for ex in range(N_EXP):
        n = jnp.sum((e == ex).astype(F32))
        counts = jnp.where(lane_e == ex, n.astype(I32), counts)
    counts_ref[...] = counts

    def partner(v, d):
        if d < LANES:
            down = pltpu.roll(v, d, 1)
            up = pltpu.roll(v, LANES - d, 1)
            return jnp.where((lane & d) != 0, down, up)
        r = d // LANES
        down = pltpu.roll(v, r, 0)
        up = pltpu.roll(v, SORT_ROWS - r, 0)
        return jnp.where((row & r) != 0, down, up)

    def bitonic_sort(keys):
        k = 2
        while k <= N_SLOTS:
            d = k // 2
            while d >= 1:
                other = partner(keys, d)
                is_low = (flat & d) == 0
                ascending = (flat & k) == 0
                take_min = is_low == ascending
                keys = jnp.where(take_min, jnp.minimum(keys, other), jnp.maximum(keys, other))
                d //= 2
            k *= 2
        return keys

    by_expert = bitonic_sort(keys) & SLOT_MASK
    tok_ref[...] = ((by_expert >> (s_bits + k_bits)) << s_bits) | (by_expert & (S - 1))
    by_slot = bitonic_sort(((by_expert - N_SLOTS // 2) << 16) | flat)
    dest_ref[...] = by_slot & SLOT_MASK


def _sort(e2d):
    return pl.pallas_call(
        _sort_kernel,
        out_shape=[jax.ShapeDtypeStruct((SORT_ROWS, LANES), I32),
                   jax.ShapeDtypeStruct((SORT_ROWS, LANES), I32),
                   jax.ShapeDtypeStruct((SUBLANES, LANES), I32)],
        compiler_params=pltpu.CompilerParams(vmem_limit_bytes=VMEM_LIMIT),
        name="route_sort",
    )(e2d)


def _sc_gather_rows(table, idx):
    n = idx.shape[0]
    per_worker = n // SC_WORKERS
    n_chunks = per_worker // SC_ROWS
    assert per_worker * SC_WORKERS == n and n_chunks * SC_ROWS == per_worker and n_chunks % 2 == 0
    mesh = plsc.VectorSubcoreMesh(core_axis_name="core", subcore_axis_name="subcore",
                                  num_cores=SC_CORES, num_subcores=SC_SUBCORES)
    row = (SC_ROWS, ROW_LINES, LANES)

    def body(table_hbm, idx_hbm, out_hbm, iv0, iv1, r0, r1, is0, is1, gs0, gs1):
        idx_v, rows, idx_sem, row_sem = (iv0, iv1), (r0, r1), (is0, is1), (gs0, gs1)
        worker = lax.axis_index("subcore") * SC_CORES + lax.axis_index("core")
        base = worker * per_worker

        def idx_copy(i, b):
            return pltpu.make_async_copy(idx_hbm.at[pl.ds(base + i * SC_ROWS, SC_ROWS)],
                                         idx_v[b], idx_sem[b])

        def row_gather(b):
            return pltpu.make_async_copy(table_hbm.at[idx_v[b]], rows[b], row_sem[b])

        idx_copy(0, 0).start()
        idx_copy(0, 0).wait()
        row_gather(0).start()
        idx_copy(1, 1).start()

        @pl.loop(0, n_chunks // 2)
        def _(pair):
            for b in (0, 1):
                i = 2 * pair + b
                row_gather(b).wait()

                @pl.when(i + 1 < n_chunks)
                def _():
                    idx_copy(i + 1, 1 - b).wait()
                    row_gather(1 - b).start()

                @pl.when(i + 2 < n_chunks)
                def _():
                    idx_copy(i + 2, b).start()

                pltpu.sync_copy(rows[b], out_hbm.at[pl.ds(base + i * SC_ROWS, SC_ROWS)])

    return pl.kernel(
        body, mesh=mesh,
        out_type=jax.ShapeDtypeStruct((n, ROW_LINES, LANES), I32),
        scratch_types=[pltpu.VMEM((SC_ROWS,), I32), pltpu.VMEM((SC_ROWS,), I32),
                       pltpu.VMEM(row, I32), pltpu.VMEM(row, I32),
                       pltpu.SemaphoreType.DMA, pltpu.SemaphoreType.DMA,
                       pltpu.SemaphoreType.DMA, pltpu.SemaphoreType.DMA],
    )(table, idx)


def _expert_kernel(bexp_ref, bp0_ref, bend_ref, bfirst_ref, bnext_ref, bslot_ref, nused_ref,
                   xs_hbm, wgu_hbm, bgu_ref, wd_hbm, bd_ref, yb_hbm,
                   xbuf, obuf, wgu_f32, wd_f32, wgu_bf, wd_bf, in_sem, out_sem, w_sem):
    j = pl.program_id(0)
    n_used = nused_ref[0]

    def weights_in(e, slot):
        return (pltpu.make_async_copy(wgu_hbm.at[e], wgu_f32.at[slot], w_sem.at[0, slot]),
                pltpu.make_async_copy(wd_hbm.at[e], wd_f32.at[slot], w_sem.at[1, slot]))

    def n_rows(jj):
        return jnp.clip(bend_ref[jj] - bp0_ref[jj], 0, BM)

    def slab_in(jj, buf):
        n = n_rows(jj)
        return pltpu.make_async_copy(xs_hbm.at[pl.ds(bp0_ref[jj], n)],
                                     xbuf.at[buf, pl.ds(0, n)], in_sem.at[buf])

    def slab_out(jj, buf):
        n = n_rows(jj)
        return pltpu.make_async_copy(obuf.at[buf, pl.ds(0, n)],
                                     yb_hbm.at[pl.ds(bp0_ref[jj], n)], out_sem.at[buf])

    @pl.when(j == 0)
    def _():
        xbuf[...] = jnp.zeros_like(xbuf)
        slab_in(0, 0).start()
        for cp in weights_in(bexp_ref[0], 0):
            cp.start()

    @pl.when(j < n_used)
    def _():
        buf = j % 2
        slab_in(j, buf).wait()

        @pl.when(j + 1 < n_used)
        def _():
            slab_in(j + 1, 1 - buf).start()

        @pl.when(bfirst_ref[j] == 1)
        def _():
            slot = bslot_ref[j]
            for cp in weights_in(bexp_ref[j], slot):
                cp.wait()

            @pl.when(bnext_ref[j] >= 0)
            def _():
                for cp in weights_in(bnext_ref[j], 1 - slot):
                    cp.start()

            wgu_bf[...] = wgu_f32[slot].astype(BF16)
            wd_bf[...] = wd_f32[slot].astype(BF16)

        lines_of = lambda ref: ref.at[buf].reshape(BM * ROW_LINES, LANES)
        x = _load_packed_rows(lines_of(xbuf)).astype(BF16)
        gu = jnp.dot(x, wgu_bf[...], preferred_element_type=F32) + bgu_ref[0]
        glu = jnp.minimum(gu[:, :D], SWIGLU_LIMIT)
        lin = jnp.clip(gu[:, D:], -SWIGLU_LIMIT, SWIGLU_LIMIT)
        act = glu * _sigmoid(SWIGLU_ALPHA * glu) * (lin + 1.0)
        y = jnp.dot(act.astype(BF16), wd_bf[...], preferred_element_type=F32) + bd_ref[0]

        @pl.when(j >= 2)
        def _():
            slab_out(j - 2, buf).wait()

        _store_packed_rows(lines_of(obuf), y)
        slab_out(j, buf).start()

        @pl.when(j == n_used - 1)
        def _():
            slab_out(j, buf).wait()

            @pl.when(j >= 1)
            def _():
                slab_out(j - 1, 1 - buf).wait()


def _experts(bexp, bp0, bend, bfirst, bnext, bslot, nused, xs, w_gu, b_gu, w_down, b_down):
    grid_spec = pltpu.PrefetchScalarGridSpec(
        num_scalar_prefetch=7,
        grid=(N_BLOCKS,),
        in_specs=[pl.BlockSpec(memory_space=pl.ANY),
                  pl.BlockSpec(memory_space=pl.ANY),
                  pl.BlockSpec((1, 1, 2 * D), lambda j, be, *_: (be[j], 0, 0)),
                  pl.BlockSpec(memory_space=pl.ANY),
                  pl.BlockSpec((1, 1, D), lambda j, be, *_: (be[j], 0, 0))],
        out_specs=pl.BlockSpec(memory_space=pl.ANY),
        scratch_shapes=[pltpu.VMEM((2, BM, ROW_LINES, LANES), I32),
                        pltpu.VMEM((2, BM, ROW_LINES, LANES), I32),
                        pltpu.VMEM((2, D, 2 * D), F32),
                        pltpu.VMEM((2, D, D), F32),
                        pltpu.VMEM((D, 2 * D), BF16),
                        pltpu.VMEM((D, D), BF16),
                        pltpu.SemaphoreType.DMA((2,)),
                        pltpu.SemaphoreType.DMA((2,)),
                        pltpu.SemaphoreType.DMA((2, 2))],
    )
    return pl.pallas_call(
        _expert_kernel,
        grid_spec=grid_spec,
        out_shape=jax.ShapeDtypeStruct((N_SLOTS, ROW_LINES, LANES), I32),
        compiler_params=pltpu.CompilerParams(
            dimension_semantics=("arbitrary",), vmem_limit_bytes=VMEM_LIMIT),
        name="experts",
    )(bexp, bp0, bend, bfirst, bnext, bslot, nused, xs, w_gu, b_gu, w_down, b_down)


def _final_kernel(y0_ref, y1_ref, y2_ref, y3_ref, grow_ref, x1_ref, mod_ref, nw_ref, *rest):
    o_ref = rest[-1]
    gr = grow_ref[0]
    gates = [gr[:, k:k + 1] for k in range(TOP_K)]
    y = sum(gates[k] * _load_packed_rows(ref)
            for k, ref in enumerate((y0_ref, y1_ref, y2_ref, y3_ref)))
    x2 = x1_ref[0] + mod_ref[0, 5:6, :] * y
    ms = jnp.mean(x2 * x2, axis=-1, keepdims=True)
    o_ref[0] = x2 * lax.rsqrt(ms + EPS) * nw_ref[...]


def _final(b0, yg, grow, x1, mod, nw, out_so_far=None):
    tm = TM_FINAL
    per_b = S // tm

    def yspec(k):
        return pl.BlockSpec((tm * ROW_LINES, LANES),
                            lambda b, i: ((b * TOP_K + k) * per_b + i, 0))

    tok = lambda n: pl.BlockSpec((1, tm, n), lambda b, i: (b, i, 0))
    in_specs = [yspec(0), yspec(1), yspec(2), yspec(3), tok(LANES), tok(D),
                pl.BlockSpec((1, 6, D), lambda b, i: (b + b0, 0, 0)),
                _resident((1, D))]
    args = [yg, yg, yg, yg, grow, x1, mod, nw]
    aliases = {}
    if out_so_far is not None:
        in_specs.append(pl.BlockSpec(memory_space=pl.ANY))
        aliases = {len(args): 0}
        args.append(out_so_far)
    return pl.pallas_call(
        _final_kernel,
        grid=(GB, per_b),
        in_specs=in_specs,
        out_specs=pl.BlockSpec((1, tm, D), lambda b, i: (b + b0, i, 0)),
        out_shape=jax.ShapeDtypeStruct((NB, S, D), F32),
        input_output_aliases=aliases,
        compiler_params=pltpu.CompilerParams(
            dimension_semantics=("arbitrary", "arbitrary"), vmem_limit_bytes=VMEM_LIMIT),
        name="final",
    )(*args)


def _block_tables(counts):
    nblk = (counts + BM - 1) // BM
    blk_cum = jnp.cumsum(nblk)
    n_used = blk_cum[-1]
    blk_start = blk_cum - nblk
    start = jnp.cumsum(counts) - counts
    jj = jnp.arange(N_BLOCKS, dtype=I32)
    jc = jnp.minimum(jj, n_used - 1)[:, None]
    owner = ((blk_start[None, :] <= jc) & (jc < blk_cum[None, :])).astype(I32)
    pick = lambda v: jnp.sum(owner * v[None, :], axis=1)
    e_j = pick(jnp.arange(N_EXP, dtype=I32))
    bp0 = pick(start) + (jj - pick(blk_start)) * BM
    bend = pick(start + counts)
    bfirst = (jj == pick(blk_start)).astype(I32)
    experts = jnp.arange(N_EXP, dtype=I32)
    used = nblk > 0
    later_used = used[None, :] & (experts[None, :] > experts[:, None])
    next_used = jnp.min(jnp.where(later_used, experts[None, :], N_EXP), axis=1)
    next_used = jnp.where(next_used == N_EXP, -1, next_used)
    slot = (jnp.cumsum(used.astype(I32)) - 1) % 2
    return (e_j, bp0.astype(I32), bend.astype(I32), bfirst, pick(next_used).astype(I32),
            pick(slot).astype(I32), n_used.reshape(1).astype(I32))


def kernel(x, c, w_ada, b_ada, norm_mix_w, w_in, conv_w, conv_b, dt_bias, a_log, d_skip,
           ssd_norm_w, w_ssd_out, w_pool, pool_scale, w_pool_out, w_out, norm_ffn_w,
           w_router, b_router, w_gu, b_gu, w_down, b_down, norm_final_w):
    assert x.shape == (NB, S, D) and w_ada.shape[0] == 1

    mod = _adaln(c, w_ada[0], b_ada).reshape(NB, 6, D)

    w_in_b = w_in[0].astype(BF16)
    o1, o2, o3, o4 = D_SSD, D_SSD + D_XBC, D_SSD + D_XBC + N_HEADS, D_SSD + D_XBC + N_HEADS + D
    wz, wx, wp, wg = w_in_b[:, :o1], w_in_b[:, o1:o2], w_in_b[:, o3:o4], w_in_b[:, o4:]
    wdt = jnp.pad(w_in_b[:, o2:o3], ((0, 0), (0, LANES - N_HEADS)))
    z, xbc, dtr, up, gts = _inproj(x, mod, norm_mix_w, wz, wx, wdt, wp, wg, conv_w[0], conv_b)

    pad_h = lambda v: jnp.pad(v[0], (0, LANES - N_HEADS))
    dtb, alog = pad_h(dt_bias), pad_h(a_log)
    head_of_chan = jnp.arange(D_SSD, dtype=I32) // HEAD_DIM
    expand = (jnp.arange(LANES, dtype=I32)[:, None] == head_of_chan[None, :]).astype(BF16)
    dskip = jnp.repeat(d_skip[0], HEAD_DIM).reshape(1, D_SSD)
    y = _ssd(xbc, z, dtr, dtb.reshape(1, LANES), alog.reshape(1, LANES),
             dtb.reshape(LANES, 1), alog.reshape(LANES, 1), dskip, ssd_norm_w, expand)

    merge_w = (w_ssd_out[0].astype(BF16), w_pool[0].astype(BF16), pool_scale,
               w_pool_out[0].astype(BF16), w_out[0].astype(BF16), norm_ffn_w,
               w_router[0].T, b_router.reshape(N_EXP, 1))
    expert_w = (w_gu[0], b_gu[0].reshape(N_EXP, 1, 2 * D), w_down[0], b_down[0].reshape(N_EXP, 1, D))

    groups = []
    for b0 in range(0, NB, GB):
        x1, h2, eidx, grow = _merge(b0, y, up, gts, x, mod, *merge_w)
        tok_of_row, row_of_slot, counts = _sort(eidx.reshape(SORT_ROWS, LANES))
        tables = _block_tables(counts[0, :N_EXP])
        xs = _sc_gather_rows(h2.reshape(GB * S, ROW_LINES, LANES), tok_of_row.reshape(N_SLOTS))
        groups.append((b0, x1, grow, tables, xs, row_of_slot))

    combined = []
    for b0, x1, grow, tables, xs, row_of_slot in groups:
        yb = _experts(*tables, xs, *expert_w)
        yg = _sc_gather_rows(yb, row_of_slot.reshape(N_SLOTS))
        combined.append((b0, x1, grow, yg))

    out = None
    for b0, x1, grow, yg in combined:
        out = _final(b0, yg.reshape(N_SLOTS * ROW_LINES, LANES), grow, x1, mod,
                     norm_final_w.reshape(1, D), out)
    return out
```
